```python
import jax, jax.numpy as jnp
from jax import lax
import numpy as np

D_MODEL = 2048
BATCH = 2
SEQ = 16384
DEPTH = 4
DEC_BATCH = 8
DEC_SEQ = 32
PAST_LEN = 2048

CHUNK = 64
D_CONV = D_MODEL
CONV_W = 31
N_HEADS = 16
N_KV_HEADS = 4
HEAD_DIM = 128
N_IDX_HEADS = 16
IDX_DIM = 64
TOPK_MAX = 256
Q_BLOCK = 128
N_GROUPS = 4
EXPERTS_PER_GROUP = 8
N_EXPERTS = N_GROUPS * EXPERTS_PER_GROUP
TOPK_INNER = 2
D_EXPERT = 512
MOE_BLOCK = 128
EPS = 1e-6
SPLITS = (D_CONV, D_CONV, N_HEADS * HEAD_DIM, N_KV_HEADS * HEAD_DIM, N_KV_HEADS * HEAD_DIM,
          N_IDX_HEADS * IDX_DIM, IDX_DIM, N_IDX_HEADS, D_MODEL, D_MODEL)
D_IN = sum(SPLITS)

kernel_name = "hybrid_conformer_dsa_hmoe_stream_step"


def rms_norm(x, g):
    xf = x.astype(jnp.float32)
    y = xf * lax.rsqrt(jnp.mean(xf * xf, axis=-1, keepdims=True) + EPS)
    return (y * g.astype(jnp.float32)).astype(x.dtype)


def layer_norm(x, g, b):
    xf = x.astype(jnp.float32)
    mu = jnp.mean(xf, axis=-1, keepdims=True)
    var = jnp.mean(jnp.square(xf - mu), axis=-1, keepdims=True)
    y = (xf - mu) * lax.rsqrt(var + EPS) * g.astype(jnp.float32) + b.astype(jnp.float32)
    return y.astype(x.dtype)


def split_proj(z):
    return jnp.split(z, [int(i) for i in np.cumsum(SPLITS)[:-1]], axis=-1)


def conv_branch(a, b, buf, w_dw, b_dw, ln_g, ln_b, w_pw):
    u = a * jax.nn.sigmoid(b)
    full = jnp.concatenate([buf.astype(u.dtype), u], axis=1)
    y = lax.conv_general_dilated(full, w_dw[:, None, :].astype(u.dtype), window_strides=(1,),
                                 padding='VALID', dimension_numbers=('NWC', 'WIO', 'NWC'),
                                 feature_group_count=D_CONV) + b_dw
    y = jax.nn.silu(layer_norm(y, ln_g, ln_b))
    return y @ w_pw, full[:, -(CONV_W - 1):]


def indexer_scores(qi, wi, ki):
    dots = jnp.einsum('bqhd,bsd->bqhs', qi, ki).astype(jnp.float32) * (IDX_DIM ** -0.5)
    w = wi.astype(jnp.float32) * (N_IDX_HEADS ** -0.5)
    return jnp.einsum('bqh,bqhs->bqs', w, jax.nn.relu(dots))


def gather_rows(rows, idx):
    return jax.vmap(lambda r, i: r[i])(rows, idx)


def sparse_attend(q, k_sel, v_sel, valid):
    b_, q_ = q.shape[:2]
    qg = q.reshape(b_, q_, N_KV_HEADS, N_HEADS // N_KV_HEADS, HEAD_DIM)
    s = jnp.einsum('bqhgd,bqnhd->bqhgn', qg, k_sel).astype(jnp.float32) * (HEAD_DIM ** -0.5)
    s = jnp.where(valid[:, :, None, None, :], s, -jnp.inf)
    p = jax.nn.softmax(s, axis=-1).astype(v_sel.dtype)
    o = jnp.einsum('bqhgn,bqnhd->bqhgd', p, v_sel)
    return o.reshape(b_, q_, N_HEADS * HEAD_DIM)


def select_and_attend(q, qi, wi, k, v, ki, mask, n_sel):
    sc = jnp.where(mask, indexer_scores(qi, wi, ki), -jnp.inf)
    vals, idx = lax.top_k(sc, n_sel)
    valid = vals > -jnp.inf
    return sparse_attend(q, gather_rows(k, idx), gather_rows(v, idx), valid)


def attn_prompt(q, qi, wi, k, v, ki):
    b_, s_ = q.shape[:2]
    nb = s_ // Q_BLOCK
    n_sel = min(TOPK_MAX, s_ // 4)
    key_pos = jnp.arange(s_)

    def blk(args):
        j, qb, qib, wib = args
        qpos = j * Q_BLOCK + jnp.arange(Q_BLOCK)
        limit = (qpos // CHUNK + 1) * CHUNK
        mask = (key_pos[None, :] < limit[:, None])[None]
        return select_and_attend(qb, qib, wib, k, v, ki, mask, n_sel)

    to_blocks = lambda t: jnp.swapaxes(t.reshape(b_, nb, Q_BLOCK, *t.shape[2:]), 0, 1)
    out = lax.map(blk, (jnp.arange(nb), to_blocks(q), to_blocks(qi), to_blocks(wi)))
    return jnp.swapaxes(out, 0, 1).reshape(b_, s_, N_HEADS * HEAD_DIM)


def attn_sample(q, qi, wi, k_all, v_all, ki_all):
    l_ = k_all.shape[1]
    n_sel = min(TOPK_MAX, l_ // 4)
    mask = jnp.ones((1, 1, l_), dtype=bool)
    return select_and_attend(q, qi, wi, k_all, v_all, ki_all, mask, n_sel)


def token_mixer(h, conv_buf, past, w_in, w_dw, b_dw, cln_g, cln_b, w_cpw, w_ao, w_out):
    b_, t_ = h.shape[:2]
    ga, gb, q, k, v, qi, ki, wi, gt_a, gt_b = split_proj(h @ w_in)
    y_conv, new_buf = conv_branch(ga, gb, conv_buf, w_dw, b_dw, cln_g, cln_b, w_cpw)
    q = q.reshape(b_, t_, N_HEADS, HEAD_DIM)
    k = k.reshape(b_, t_, N_KV_HEADS, HEAD_DIM)
    v = v.reshape(b_, t_, N_KV_HEADS, HEAD_DIM)
    qi = qi.reshape(b_, t_, N_IDX_HEADS, IDX_DIM)
    if past is None:
        o = attn_prompt(q, qi, wi, k, v, ki)
    else:
        pk, pv, pki = past
        o = attn_sample(q, qi, wi,
                        jnp.concatenate([pk.astype(k.dtype), k], axis=1),
                        jnp.concatenate([pv.astype(v.dtype), v], axis=1),
                        jnp.concatenate([pki.astype(ki.dtype), ki], axis=1))
    y_attn = o @ w_ao
    m = jax.nn.sigmoid(gt_a) * y_conv + jax.nn.sigmoid(gt_b) * y_attn
    return m @ w_out, new_buf, k, v, ki


def moe(h, r_g, r_gb, r_e, r_eb, w_g, w_u, w_d):
    shp = h.shape
    x = h.reshape(-1, D_MODEL)
    n = x.shape[0]
    lg = (x @ r_g).astype(jnp.float32) + r_gb.astype(jnp.float32)
    pg = jax.nn.softmax(lg, axis=-1)
    g_sel = jnp.argmax(lg, axis=-1)
    p_group = jnp.take_along_axis(pg, g_sel[:, None], axis=-1)
    le = ((x @ r_e).astype(jnp.float32) + r_eb.astype(jnp.float32)).reshape(n, N_GROUPS, EXPERTS_PER_GROUP)
    le = jnp.take_along_axis(le, g_sel[:, None, None], axis=1)[:, 0]
    top_p, top_i = lax.top_k(jax.nn.softmax(le, axis=-1), TOPK_INNER)
    top_p = top_p / jnp.sum(top_p, axis=-1, keepdims=True)
    wts = (p_group * top_p).astype(x.dtype)
    eid = g_sel[:, None] * EXPERTS_PER_GROUP + top_i
    a_ = n * TOPK_INNER
    e_flat = eid.reshape(-1).astype(jnp.int32)
    t_flat = jnp.repeat(jnp.arange(n, dtype=jnp.int32), TOPK_INNER)
    w_flat = wts.reshape(-1)
    order = jnp.argsort(e_flat)
    e_s, t_s, w_s = e_flat[order], t_flat[order], w_flat[order]
    counts = jnp.bincount(e_flat, length=N_EXPERTS)
    start = jnp.cumsum(counts) - counts
    padded = (counts + MOE_BLOCK - 1) // MOE_BLOCK * MOE_BLOCK
    pad_end = jnp.cumsum(padded)
    pad_start = pad_end - padded
    dest = pad_start[e_s] + jnp.arange(a_, dtype=jnp.int32) - start[e_s]
    nblk = -(-a_ // MOE_BLOCK) + N_EXPERTS
    tok_buf = jnp.full((nblk * MOE_BLOCK,), n, jnp.int32).at[dest].set(t_s)
    blk_e = jnp.minimum(jnp.searchsorted(pad_end, jnp.arange(nblk) * MOE_BLOCK, side='right'),
                        N_EXPERTS - 1)
    x_pad = jnp.concatenate([x, jnp.zeros((1, D_MODEL), x.dtype)], axis=0)
    xb = x_pad[tok_buf].reshape(nblk, MOE_BLOCK, D_MODEL)

    def expert_block(args):
        xe, e = args
        return (jax.nn.silu(xe @ w_g[e]) * (xe @ w_u[e])) @ w_d[e]

    yb = lax.map(expert_block, (xb, blk_e)).reshape(nblk * MOE_BLOCK, D_MODEL)
    out = jax.ops.segment_sum(yb[dest] * w_s[:, None], t_s, num_segments=n)
    return out.reshape(shp)


def trunk_layer(x, c, conv_buf, past, norm1_g, norm2_g, w_mod, b_mod, w_in, w_dw, b_dw, cln_g, cln_b,
                w_cpw, w_ao, w_out, r_g, r_gb, r_e, r_eb, w_g, w_u, w_d):
    mod = jax.nn.silu(c) @ w_mod + b_mod
    sh1, sc1, g1, sh2, sc2, g2 = jnp.split(mod[:, None, :], 6, axis=-1)
    h = rms_norm(x, norm1_g) * (1 + sc1) + sh1
    mix, new_buf, k, v, ki = token_mixer(h, conv_buf, past, w_in, w_dw, b_dw, cln_g, cln_b, w_cpw, w_ao, w_out)
    x = x + g1 * mix
    h = rms_norm(x, norm2_g) * (1 + sc2) + sh2
    x = x + g2 * moe(h, r_g, r_gb, r_e, r_eb, w_g, w_u, w_d)
    return x, new_buf, k, v, ki


def setup_inputs(seed: int = 0) -> dict:
    key = jax.random.key(seed)
    ks = iter(jax.random.split(key, 32))
    nrm = lambda shape, scale: jax.random.normal(next(ks), shape, jnp.float32) * scale
    d = D_MODEL
    return {
        "x_prompt": nrm((BATCH, SEQ, d), 1.0),
        "x_sample": nrm((DEC_BATCH, DEC_SEQ, d), 1.0),
        "c_prompt": nrm((BATCH, d), 1.0),
        "c_sample": nrm((DEC_BATCH, d), 1.0),
        "cache_k": nrm((DEPTH, DEC_BATCH, PAST_LEN, N_KV_HEADS, HEAD_DIM), 1.0),
        "cache_v": nrm((DEPTH, DEC_BATCH, PAST_LEN, N_KV_HEADS, HEAD_DIM), 1.0),
        "cache_kidx": nrm((DEPTH, DEC_BATCH, PAST_LEN, IDX_DIM), 1.0),
        "state_conv": nrm((DEPTH, DEC_BATCH, CONV_W - 1, D_CONV), 0.5),
        "norm1_g": 1.0 + nrm((DEPTH, d), 0.01),
        "norm2_g": 1.0 + nrm((DEPTH, d), 0.01),
        "w_mod": nrm((DEPTH, d, 6 * d), 0.5 * d ** -0.5),
        "b_mod": nrm((DEPTH, 6 * d), 0.01),
        "w_in": nrm((DEPTH, d, D_IN), d ** -0.5),
        "conv_dw_w": nrm((DEPTH, CONV_W, D_CONV), CONV_W ** -0.5),
        "conv_dw_b": nrm((DEPTH, D_CONV), 0.01),
        "conv_ln_g": 1.0 + nrm((DEPTH, D_CONV), 0.01),
        "conv_ln_b": nrm((DEPTH, D_CONV), 0.01),
        "w_conv_out": nrm((DEPTH, D_CONV, d), D_CONV ** -0.5),
        "w_attn_out": nrm((DEPTH, N_HEADS * HEAD_DIM, d), (N_HEADS * HEAD_DIM) ** -0.5),
        "w_out": nrm((DEPTH, d, d), d ** -0.5),
        "router_g": nrm((DEPTH, d, N_GROUPS), d ** -0.5),
        "router_g_b": nrm((DEPTH, N_GROUPS), 0.01),
        "router_e": nrm((DEPTH, d, N_EXPERTS), d ** -0.5),
        "router_e_b": nrm((DEPTH, N_EXPERTS), 0.01),
        "w_gate": nrm((DEPTH, N_EXPERTS, d, D_EXPERT), d ** -0.5),
        "w_up": nrm((DEPTH, N_EXPERTS, d, D_EXPERT), d ** -0.5),
        "w_down": nrm((DEPTH, N_EXPERTS, D_EXPERT, d), D_EXPERT ** -0.5),
        "final_g": 1.0 + nrm((d,), 0.01),
    }


def reference(x_prompt, x_sample, c_prompt, c_sample, cache_k, cache_v, cache_kidx, state_conv,
              norm1_g, norm2_g, w_mod, b_mod, w_in, conv_dw_w, conv_dw_b, conv_ln_g, conv_ln_b,
              w_conv_out, w_attn_out, w_out, router_g, router_g_b, router_e, router_e_b,
              w_gate, w_up, w_down, final_g):
    xp, xs = x_prompt, x_sample
    n_keep = min(x_prompt.shape[1], cache_k.shape[2])
    kp, vp, kip, cbp, ksl, vsl, kisl, cbs = [], [], [], [], [], [], [], []
    zero_buf = jnp.zeros((x_prompt.shape[0], CONV_W - 1, D_CONV), x_prompt.dtype)
    for l in range(DEPTH):
        w = (norm1_g[l], norm2_g[l], w_mod[l], b_mod[l], w_in[l], conv_dw_w[l], conv_dw_b[l],
             conv_ln_g[l], conv_ln_b[l], w_conv_out[l], w_attn_out[l], w_out[l], router_g[l],
             router_g_b[l], router_e[l], router_e_b[l], w_gate[l], w_up[l], w_down[l])
        xp, buf_p, k_p, v_p, ki_p = trunk_layer(xp, c_prompt, zero_buf, None, *w)
        xs, buf_s, k_s, v_s, ki_s = trunk_layer(xs, c_sample, state_conv[l],
                                                (cache_k[l], cache_v[l], cache_kidx[l]), *w)
        kp.append(k_p[:, -n_keep:]); vp.append(v_p[:, -n_keep:]); kip.append(ki_p[:, -n_keep:])
        cbp.append(buf_p)
        ksl.append(k_s); vsl.append(v_s); kisl.append(ki_s); cbs.append(buf_s)
    y_prompt = rms_norm(xp, final_g)
    y_sample = rms_norm(xs, final_g)
    return (y_prompt, y_sample, jnp.stack(kp), jnp.stack(vp), jnp.stack(kip), jnp.stack(cbp),
            jnp.stack(ksl), jnp.stack(vsl), jnp.stack(kisl), jnp.stack(cbs))
```

```python
import functools

import jax
import jax.numpy as jnp
from jax import lax
from jax.experimental import pallas as pl
from jax.experimental.pallas import tpu as pltpu

F32 = jnp.float32
BF16 = jnp.bfloat16
I32 = jnp.int32

D_MODEL = 2048
CHUNK = 64
CHUNK_SHIFT = 6
CONV_W = 31
N_HEADS = 16
N_KV_HEADS = 4
HEAD_DIM = 128
N_IDX_HEADS = 16
IDX_DIM = 64
TOPK_MAX = 256
N_GROUPS = 4
EXPERTS_PER_GROUP = 8
N_EXPERTS = N_GROUPS * EXPERTS_PER_GROUP
D_EXPERT = 512
EPS = 1e-6

LANES = 128
BF16_ROWS = 16
VMEM_LIMIT = 60 * 1024 * 1024

C_GA, C_GB, C_Q, C_GTA, C_GTB = 0, 2048, 4096, 6144, 8192
C_QI, C_K, C_V, C_KIWI = 10240, 11264, 11776, 12288
N_COLS = 12800
HALO = 32

NEG_BIG = -1e30
INT_MIN = -(2 ** 31)
NEG_INF_KEY = INT_MIN + 0x7FFFFF


def _cparams(sem):
    return pltpu.CompilerParams(dimension_semantics=sem, vmem_limit_bytes=VMEM_LIMIT)


def _sigmoid(x):
    return 1.0 / (1.0 + jnp.exp(-x))


def _silu(x):
    return x * _sigmoid(x)


def _resident(shape, index_map):
    return pl.BlockSpec(shape, index_map, pipeline_mode=pl.Buffered(1))


def _mod_kernel(c_ref, w_ref, b_ref, o_ref):
    s = _silu(c_ref[...]).astype(BF16)
    o_ref[0] = jnp.dot(s, w_ref[0].astype(BF16), preferred_element_type=F32) + b_ref[0]


def _modulation(c_all, w_mod, b_mod):
    depth, d, n6 = w_mod.shape
    rows = c_all.shape[0]
    tn = 1024
    return pl.pallas_call(
        _mod_kernel,
        grid=(depth, n6 // tn),
        in_specs=[pl.BlockSpec((rows, d), lambda l, j: (0, 0)),
                  pl.BlockSpec((1, d, tn), lambda l, j: (l, 0, j)),
                  pl.BlockSpec((1, 1, tn), lambda l, j: (l, 0, j))],
        out_specs=pl.BlockSpec((1, rows, tn), lambda l, j: (l, 0, j)),
        out_shape=jax.ShapeDtypeStruct((depth, rows, n6), F32),
        compiler_params=_cparams(("arbitrary", "arbitrary")),
        name="modulation",
    )(c_all, w_mod, b_mod.reshape(depth, 1, n6))


def _mod_spec(row_mod, tm, k, grid_rank):
    if grid_rank == 3:
        if row_mod:
            return pl.BlockSpec((1, tm, D_MODEL), lambda b, i, n: (b, i, k))
        return pl.BlockSpec((1, 1, D_MODEL), lambda b, i, n: (b, 0, k))
    if row_mod:
        return pl.BlockSpec((1, tm, D_MODEL), lambda b, i: (b, i, k))
    return pl.BlockSpec((1, 1, D_MODEL), lambda b, i: (b, 0, k))


def _rms(x, g):
    ms = jnp.mean(x * x, axis=-1, keepdims=True)
    return x * lax.rsqrt(ms + EPS) * g


def _inproj_kernel(*refs, has_resid):
    if has_resid:
        x_ref, r_ref, gate_ref, sh_ref, sc_ref, g_ref, w_ref, z_ref, xo_ref, h_ref = refs
    else:
        x_ref, sh_ref, sc_ref, g_ref, w_ref, z_ref, h_ref = refs

    @pl.when(pl.program_id(2) == 0)
    def _():
        x = x_ref[0]
        if has_resid:
            x = x + gate_ref[0] * r_ref[0]
            xo_ref[0] = x
        h = _rms(x, g_ref[...]) * (1.0 + sc_ref[0]) + sh_ref[0]
        h_ref[...] = h.astype(BF16)

    z_ref[0] = jnp.dot(h_ref[...], w_ref[...], preferred_element_type=F32)


def _inproj(x, resid, mod, prev_mod, row_mod, g, w, tm):
    nb, t, d = x.shape
    tn = 1280
    has_resid = resid is not None
    xspec = pl.BlockSpec((1, tm, d), lambda b, i, n: (b, i, 0))
    in_specs = [xspec]
    args = [x]
    if has_resid:
        in_specs += [xspec, _mod_spec(row_mod, tm, 5, 3)]
        args += [resid, prev_mod]
    in_specs += [_mod_spec(row_mod, tm, 0, 3), _mod_spec(row_mod, tm, 1, 3),
                 pl.BlockSpec((1, d), lambda b, i, n: (0, 0)),
                 pl.BlockSpec((d, tn), lambda b, i, n: (0, n))]
    args += [mod, mod, g.reshape(1, d), w]
    zspec = pl.BlockSpec((1, tm, tn), lambda b, i, n: (b, i, n))
    zshape = jax.ShapeDtypeStruct((nb, t, N_COLS), F32)
    if has_resid:
        out_specs, out_shape = (zspec, xspec), (zshape, jax.ShapeDtypeStruct((nb, t, d), F32))
    else:
        out_specs, out_shape = zspec, zshape
    out = pl.pallas_call(
        functools.partial(_inproj_kernel, has_resid=has_resid),
        grid=(nb, t // tm, N_COLS // tn),
        in_specs=in_specs, out_specs=out_specs, out_shape=out_shape,
        scratch_shapes=[pltpu.VMEM((tm, d), BF16)],
        compiler_params=_cparams(("arbitrary", "arbitrary", "arbitrary")),
        name="inproj",
    )(*args)
    return out if has_resid else (out, x)


def _conv_kernel(a_ref, b_ref, ha_ref, hb_ref, init_ref, wdw_ref, bdw_ref, lng_ref, lnb_ref,
                 wpw_ref, y_ref, tail_ref, full_ref, cv_ref, *, tm):
    i = pl.program_id(1)
    full_ref[HALO:HALO + tm, :] = a_ref[0] * _sigmoid(b_ref[0])

    @pl.when(i == 0)
    def _():
        full_ref[0:HALO, :] = init_ref[0]

    @pl.when(i > 0)
    def _():
        full_ref[0:HALO, :] = ha_ref[0] * _sigmoid(hb_ref[0])

    tail_ref[0] = full_ref[tm:tm + HALO, :]

    cc = 512
    rb = min(tm, 64)
    off = HALO - (CONV_W - 1)
    for c0 in range(0, D_MODEL, cc):
        for r0 in range(0, tm, rb):
            acc = jnp.zeros((rb, cc), F32)
            for j in range(CONV_W):
                acc = acc + wdw_ref[j:j + 1, c0:c0 + cc] * full_ref[r0 + off + j:r0 + off + j + rb, c0:c0 + cc]
            cv_ref[r0:r0 + rb, c0:c0 + cc] = acc + bdw_ref[:, c0:c0 + cc]

    y = cv_ref[...]
    mu = jnp.mean(y, axis=-1, keepdims=True)
    yc = y - mu
    var = jnp.mean(yc * yc, axis=-1, keepdims=True)
    yn = yc * lax.rsqrt(var + EPS) * lng_ref[...] + lnb_ref[...]
    y_ref[0] = jnp.dot(_silu(yn).astype(BF16), wpw_ref[...], preferred_element_type=F32)


def _conv_branch(z, init_buf, w_dw, b_dw, ln_g, ln_b, w_pw, tm):
    nb, t, _ = z.shape
    d = D_MODEL
    hb = tm // HALO
    tile = lambda k: pl.BlockSpec((1, tm, d), lambda b, i: (b, i, k))
    halo = lambda k: pl.BlockSpec((1, HALO, d), lambda b, i: (b, jnp.maximum(i * hb - 1, 0), k))
    vec = pl.BlockSpec((1, d), lambda b, i: (0, 0))
    return pl.pallas_call(
        functools.partial(_conv_kernel, tm=tm),
        grid=(nb, t // tm),
        in_specs=[tile(C_GA // d), tile(C_GB // d), halo(C_GA // d), halo(C_GB // d),
                  pl.BlockSpec((1, HALO, d), lambda b, i: (b, 0, 0)),
                  pl.BlockSpec((HALO, d), lambda b, i: (0, 0)),
                  vec, vec, vec, _resident((d, d), lambda b, i: (0, 0))],
        out_specs=(pl.BlockSpec((1, tm, d), lambda b, i: (b, i, 0)),
                   pl.BlockSpec((1, HALO, d), lambda b, i: (b, 0, 0))),
        out_shape=(jax.ShapeDtypeStruct((nb, t, d), F32),
                   jax.ShapeDtypeStruct((nb, HALO, d), F32)),
        scratch_shapes=[pltpu.VMEM((tm + HALO, d), F32), pltpu.VMEM((tm, d), F32)],
        compiler_params=_cparams(("arbitrary", "arbitrary")),
        name="conv_branch",
    )(z, z, z, z, init_buf, w_dw, b_dw.reshape(1, d), ln_g.reshape(1, d), ln_b.reshape(1, d), w_pw)


def _attn_kernel(q_ref, qi_ref, kw_ref, k_ref, v_ref, kit_ref, o_ref,
                 keys_ref, qs_ref, qis_ref, wb_ref, m_ref, l_ref, acc_ref,
                 *, tq, kb, s_pad, causal, s_valid, n_sel):
    i = pl.program_id(1)
    q0 = i * tq
    gq = N_HEADS // N_KV_HEADS
    nlb = kb // LANES
    if causal:
        nkb = (q0 + tq + kb - 1) // kb
    else:
        nkb = s_pad // kb

    for h in range(N_HEADS):
        g, j = divmod(h, gq)
        qs_ref[g, j * tq:(j + 1) * tq, :] = q_ref[0, :, h * HEAD_DIM:(h + 1) * HEAD_DIM].astype(BF16)
    kw = kw_ref[0]
    for h in range(N_IDX_HEADS):
        qis_ref[h * tq:(h + 1) * tq, :] = (
            qi_ref[0, :, h * IDX_DIM:(h + 1) * IDX_DIM] * (IDX_DIM ** -0.5)).astype(BF16)
        wcol = kw[:, IDX_DIM + h:IDX_DIM + h + 1] * (N_IDX_HEADS ** -0.5)
        wb_ref[h] = jnp.broadcast_to(wcol, (tq, LANES))

    rows = q0 + lax.broadcasted_iota(I32, (tq, 1), 0)
    if causal:
        limit = ((rows >> CHUNK_SHIFT) + 1) << CHUNK_SHIFT
    else:
        limit = jnp.full((tq, 1), s_valid, I32)

    def score_block(b, carry):
        c0 = pl.multiple_of(b * kb, kb)
        d = jnp.dot(qis_ref[...], kit_ref[0, :, pl.ds(c0, kb)], preferred_element_type=F32)
        acc = jnp.zeros((tq, kb), F32)
        for h in range(N_IDX_HEADS):
            w = jnp.concatenate([wb_ref[h]] * nlb, axis=1)
            acc = acc + w * jnp.maximum(d[h * tq:(h + 1) * tq], 0.0)
        pos = c0 + lax.broadcasted_iota(I32, (1, kb), 1)
        acc = jnp.where(pos < limit, acc, -jnp.inf)
        bits = pltpu.bitcast(acc, I32)
        keys_ref[:, pl.ds(c0, kb)] = bits ^ ((bits >> 31) & 0x7FFFFFFF)
        return carry

    lax.fori_loop(0, nkb, score_block, 0)

    def count_ge(thr):
        def body(b, c):
            blk = keys_ref[:, pl.ds(pl.multiple_of(b * kb, kb), kb)]
            hit = jnp.where(blk >= thr, 1.0, 0.0)
            for u in range(nlb):
                c = c + hit[:, u * LANES:(u + 1) * LANES]
            return c
        c = lax.fori_loop(0, nkb, body, jnp.zeros((tq, LANES), F32))
        return jnp.sum(c, axis=1, keepdims=True)

    def bit_step(it, ans):
        cand = ans + (jnp.int32(1) << (31 - it))
        return jnp.where(count_ge(cand) >= float(n_sel), cand, ans)

    kth = lax.fori_loop(0, 32, bit_step, jnp.full((tq, 1), INT_MIN, I32))
    thr = jnp.maximum(kth, NEG_INF_KEY + 1)
    excess = count_ge(thr) - float(n_sel)

    @pl.when(jnp.max(excess) > 0.0)
    def _():
        def count_eq(b, c):
            blk = keys_ref[:, pl.ds(pl.multiple_of(b * kb, kb), kb)]
            hit = jnp.where(blk == thr, 1.0, 0.0)
            for u in range(nlb):
                c = c + hit[:, u * LANES:(u + 1) * LANES]
            return c
        n_eq = jnp.sum(lax.fori_loop(0, nkb, count_eq, jnp.zeros((tq, LANES), F32)),
                       axis=1, keepdims=True)
        keep = jnp.where(excess > 0.0, n_eq - excess, n_eq)
        tri = jnp.where(lax.broadcasted_iota(I32, (LANES, LANES), 0)
                        <= lax.broadcasted_iota(I32, (LANES, LANES), 1), 1.0, 0.0).astype(BF16)

        def drop_block(b, run):
            c0 = pl.multiple_of(b * LANES, LANES)
            blk = keys_ref[:, pl.ds(c0, LANES)]
            eq = blk == thr
            eqf = jnp.where(eq, 1.0, 0.0)
            rank = run + jnp.dot(eqf.astype(BF16), tri, preferred_element_type=F32) - 1.0
            keys_ref[:, pl.ds(c0, LANES)] = jnp.where(eq & (rank >= keep), INT_MIN, blk)
            return run + jnp.sum(eqf, axis=1, keepdims=True)

        lax.fori_loop(0, nkb * nlb, drop_block, jnp.zeros((tq, 1), F32))

    m_ref[...] = jnp.full(m_ref.shape, NEG_BIG, F32)
    l_ref[...] = jnp.zeros(l_ref.shape, F32)
    acc_ref[...] = jnp.zeros(acc_ref.shape, F32)
    scale = HEAD_DIM ** -0.5

    def attend_block(b, carry):
        c0 = pl.multiple_of(b * kb, kb)
        bias = jnp.where(keys_ref[:, pl.ds(c0, kb)] >= thr, 0.0, NEG_BIG)
        bias = jnp.concatenate([bias] * gq, axis=0)
        for g in range(N_KV_HEADS):
            kg = k_ref[0, pl.ds(c0, kb), g * HEAD_DIM:(g + 1) * HEAD_DIM]
            vg = v_ref[0, pl.ds(c0, kb), g * HEAD_DIM:(g + 1) * HEAD_DIM]
            s = lax.dot_general(qs_ref[g], kg, (((1,), (1,)), ((), ())),
                                preferred_element_type=F32) * scale + bias
            m_prev = m_ref[g]
            m_new = jnp.maximum(m_prev, jnp.max(s, axis=1, keepdims=True))
            alpha = jnp.exp(m_prev - m_new)
            p = jnp.exp(s - jnp.concatenate([m_new] * nlb, axis=1))
            l_ref[g] = alpha * l_ref[g] + jnp.sum(p, axis=1, keepdims=True)
            acc_ref[g] = alpha * acc_ref[g] + jnp.dot(p.astype(BF16), vg, preferred_element_type=F32)
            m_ref[g] = m_new
        return carry

    lax.fori_loop(0, nkb, attend_block, 0)

    for h in range(N_HEADS):
        g, j = divmod(h, gq)
        o = acc_ref[g, j * tq:(j + 1) * tq, :] / l_ref[g, j * tq:(j + 1) * tq, :]
        o_ref[0, :, h * HEAD_DIM:(h + 1) * HEAD_DIM] = o.astype(BF16)


def _attention(z, kb16, vb16, kit, tq, kb, causal, s_valid):
    nb, t, _ = z.shape
    s_pad = kb16.shape[1]
    dq = N_HEADS * HEAD_DIM
    dqi = N_IDX_HEADS * IDX_DIM
    dkv = N_KV_HEADS * HEAD_DIM
    gq = N_HEADS // N_KV_HEADS
    return pl.pallas_call(
        functools.partial(_attn_kernel, tq=tq, kb=kb, s_pad=s_pad, causal=causal, s_valid=s_valid,
                          n_sel=min(TOPK_MAX, s_valid // 4)),
        grid=(nb, t // tq),
        in_specs=[pl.BlockSpec((1, tq, dq), lambda b, i: (b, i, C_Q // dq)),
                  pl.BlockSpec((1, tq, dqi), lambda b, i: (b, i, C_QI // dqi)),
                  pl.BlockSpec((1, tq, LANES), lambda b, i: (b, i, C_KIWI // LANES)),
                  _resident((1, s_pad, dkv), lambda b, i: (b, 0, 0)),
                  _resident((1, s_pad, dkv), lambda b, i: (b, 0, 0)),
                  _resident((1, IDX_DIM, s_pad), lambda b, i: (b, 0, 0))],
        out_specs=pl.BlockSpec((1, tq, dq), lambda b, i: (b, i, 0)),
        out_shape=jax.ShapeDtypeStruct((nb, t, dq), BF16),
        scratch_shapes=[pltpu.VMEM((tq, s_pad), I32),
                        pltpu.VMEM((N_KV_HEADS, gq * tq, HEAD_DIM), BF16),
                        pltpu.VMEM((N_IDX_HEADS * tq, IDX_DIM), BF16),
                        pltpu.VMEM((N_IDX_HEADS, tq, LANES), F32),
                        pltpu.VMEM((N_KV_HEADS, gq * tq, LANES), F32),
                        pltpu.VMEM((N_KV_HEADS, gq * tq, LANES), F32),
                        pltpu.VMEM((N_KV_HEADS, gq * tq, HEAD_DIM), F32)],
        compiler_params=_cparams(("arbitrary", "arbitrary")),
        name="sparse_attention",
    )(z, z, z, kb16, vb16, kit)


def _post_kernel(o_ref, yc_ref, ga_ref, gb_ref, x_ref, g1_ref, sh2_ref, sc2_ref, n2_ref,
                 wao_ref, wout_ref, xm_ref, h2_ref):
    ya = jnp.dot(o_ref[0], wao_ref[...], preferred_element_type=F32)
    m = _sigmoid(ga_ref[0]) * yc_ref[0] + _sigmoid(gb_ref[0]) * ya
    mix = jnp.dot(m.astype(BF16), wout_ref[...], preferred_element_type=F32)
    x = x_ref[0] + g1_ref[0] * mix
    xm_ref[0] = x
    h2_ref[0] = (_rms(x, n2_ref[...]) * (1.0 + sc2_ref[0]) + sh2_ref[0]).astype(BF16)


def _post(o, y_conv, z, x, mod, row_mod, n2_g, w_ao, w_out, tm):
    nb, t, d = x.shape
    tile = lambda k: pl.BlockSpec((1, tm, d), lambda b, i: (b, i, k))
    return pl.pallas_call(
        _post_kernel,
        grid=(nb, t // tm),
        in_specs=[tile(0), tile(0), tile(C_GTA // d), tile(C_GTB // d), tile(0),
                  _mod_spec(row_mod, tm, 2, 2), _mod_spec(row_mod, tm, 3, 2),
                  _mod_spec(row_mod, tm, 4, 2),
                  pl.BlockSpec((1, d), lambda b, i: (0, 0)),
                  _resident((d, d), lambda b, i: (0, 0)),
                  _resident((d, d), lambda b, i: (0, 0))],
        out_specs=(tile(0), tile(0)),
        out_shape=(jax.ShapeDtypeStruct((nb, t, d), F32), jax.ShapeDtypeStruct((nb, t, d), BF16)),
        compiler_params=_cparams(("arbitrary", "arbitrary")),
        name="merge_out_proj",
    )(o, y_conv, z, z, x, mod, mod, mod, n2_g.reshape(1, d), w_ao, w_out)


def _router_kernel(h_ref, wrt_ref, rb_ref, dest_ref, wts_ref, meta_ref, *, tm):
    lt = lax.dot_general(wrt_ref[...], h_ref[...], (((1,), (1,)), ((), ())),
                         preferred_element_type=F32) + rb_ref[...][:, 0:1]
    row = lambda r: lt[r:r + 1, :]
    best, gi = row(0), jnp.zeros((1, tm), I32)
    for k in range(1, N_GROUPS):
        upd = row(k) > best
        best = jnp.where(upd, row(k), best)
        gi = jnp.where(upd, k, gi)
    sg = jnp.zeros((1, tm), F32)
    for k in range(N_GROUPS):
        sg = sg + jnp.exp(row(k) - best)
    p_group = 1.0 / sg

    le = []
    for j in range(EXPERTS_PER_GROUP):
        v = row(N_GROUPS + j)
        for g in range(1, N_GROUPS):
            v = jnp.where(gi == g, row(N_GROUPS + g * EXPERTS_PER_GROUP + j), v)
        le.append(v)
    b1, i1 = le[0], jnp.zeros((1, tm), I32)
    for j in range(1, EXPERTS_PER_GROUP):
        upd = le[j] > b1
        b1 = jnp.where(upd, le[j], b1)
        i1 = jnp.where(upd, j, i1)
    b2, i2 = jnp.full((1, tm), -jnp.inf, F32), jnp.zeros((1, tm), I32)
    for j in range(EXPERTS_PER_GROUP):
        upd = jnp.where(i1 == j, -jnp.inf, le[j]) > b2
        b2 = jnp.where(upd, le[j], b2)
        i2 = jnp.where(upd, j, i2)
    p2 = jnp.exp(b2 - b1)
    w0 = p_group / (1.0 + p2)
    w1 = p_group * p2 / (1.0 + p2)
    e0 = gi * EXPERTS_PER_GROUP + i1
    e1 = gi * EXPERTS_PER_GROUP + i2

    eio = lax.broadcasted_iota(I32, (N_EXPERTS, tm), 0)
    hit0 = jnp.where(eio == e0, 1.0, 0.0)
    hit1 = jnp.where(eio == e1, 1.0, 0.0)
    et = hit0 + hit1
    counts = jnp.sum(et, axis=1, keepdims=True)
    units = jnp.floor((counts + (BF16_ROWS - 1)) * (1.0 / BF16_ROWS))
    lower = jnp.where(lax.broadcasted_iota(I32, (N_EXPERTS, N_EXPERTS), 1)
                      < lax.broadcasted_iota(I32, (N_EXPERTS, N_EXPERTS), 0), 1.0, 0.0)
    starts = jnp.dot(lower.astype(BF16), jnp.broadcast_to(units, (N_EXPERTS, LANES)).astype(BF16),
                     preferred_element_type=F32) * float(BF16_ROWS)
    before = jnp.where(lax.broadcasted_iota(I32, (tm, tm), 0)
                       < lax.broadcasted_iota(I32, (tm, tm), 1), 1.0, 0.0).astype(BF16)
    ranks = jnp.dot(et.astype(BF16), before, preferred_element_type=F32)
    slot = starts[:, 0:1] + ranks
    d0 = jnp.sum(hit0 * slot, axis=0, keepdims=True)
    d1 = jnp.sum(hit1 * slot, axis=0, keepdims=True)
    dest_ref[0] = jnp.zeros((8, tm), I32)
    dest_ref[0, 0:1, :] = d0.astype(I32)
    dest_ref[0, 1:2, :] = d1.astype(I32)
    wts_ref[0] = jnp.zeros((8, tm), F32)
    wts_ref[0, 0:1, :] = w0
    wts_ref[0, 1:2, :] = w1
    meta_ref[0, 0:N_EXPERTS, :] = starts.astype(I32)
    meta_ref[0, N_EXPERTS:2 * N_EXPERTS, :] = jnp.broadcast_to(counts, (N_EXPERTS, LANES)).astype(I32)


def _router(h2, wrt, rbias, tm):
    n, d = h2.shape
    nt = n // tm
    return pl.pallas_call(
        functools.partial(_router_kernel, tm=tm),
        grid=(nt,),
        in_specs=[pl.BlockSpec((tm, d), lambda i: (i, 0)),
                  pl.BlockSpec((LANES, d), lambda i: (0, 0)),
                  pl.BlockSpec((LANES, LANES), lambda i: (0, 0))],
        out_specs=(pl.BlockSpec((1, 8, tm), lambda i: (i, 0, 0)),
                   pl.BlockSpec((1, 8, tm), lambda i: (i, 0, 0)),
                   pl.BlockSpec((1, 2 * N_EXPERTS, LANES), lambda i: (i, 0, 0))),
        out_shape=(jax.ShapeDtypeStruct((nt, 8, tm), I32),
                   jax.ShapeDtypeStruct((nt, 8, tm), F32),
                   jax.ShapeDtypeStruct((nt, 2 * N_EXPERTS, LANES), I32)),
        compiler_params=_cparams(("arbitrary",)),
        name="moe_router",
    )(h2, wrt, rbias)


def _experts_kernel(meta_ref, h_ref, dest_ref, wts_ref, wg_ref, wu_ref, wd_ref, o_ref,
                    dall_ref, xs_ref, ws_ref, *, tm, nslot, rb):
    t = pl.program_id(0)
    e = pl.program_id(1)
    sb = 256

    @pl.when(e == 0)
    def _():
        d0 = dest_ref[0, 0:1, :]
        d1 = dest_ref[0, 1:2, :]
        w0 = wts_ref[0, 0:1, :]
        w1 = wts_ref[0, 1:2, :]
        for s0 in range(0, nslot, sb):
            sio = s0 + lax.broadcasted_iota(I32, (sb, 1), 0)
            m0 = sio == d0
            m1 = sio == d1
            dd = (jnp.where(m0, 1.0, 0.0) + jnp.where(m1, 1.0, 0.0)).astype(BF16)
            dall_ref[s0:s0 + sb, :] = dd
            wcol = jnp.sum(jnp.where(m0, w0, 0.0) + jnp.where(m1, w1, 0.0), axis=1, keepdims=True)
            ws_ref[s0:s0 + sb, :] = jnp.broadcast_to(wcol, (sb, LANES))
            xs_ref[s0:s0 + sb, :] = jnp.dot(dd, h_ref[...], preferred_element_type=F32).astype(BF16)
        xs_ref[nslot:nslot + rb, :] = jnp.zeros((rb, D_MODEL), BF16)
        ws_ref[nslot:nslot + rb, :] = jnp.zeros((rb, LANES), F32)

    start = meta_ref[t, e]
    n = meta_ref[t, N_EXPERTS + e]

    def run_block(j, carry):
        r0 = pl.multiple_of(start + j * rb, BF16_ROWS)
        xe = xs_ref[pl.ds(r0, rb), :]
        g = jnp.dot(xe, wg_ref[0], preferred_element_type=F32)
        u = jnp.dot(xe, wu_ref[0], preferred_element_type=F32)
        y = jnp.dot((_silu(g) * u).astype(BF16), wd_ref[0], preferred_element_type=F32)
        y = y * jnp.concatenate([ws_ref[pl.ds(r0, rb), :]] * (D_MODEL // LANES), axis=1)
        valid = lax.broadcasted_iota(I32, (rb, 1), 0) < n - j * rb
        xs_ref[pl.ds(r0, rb), :] = jnp.where(valid, y.astype(BF16), xe)
        return carry

    lax.fori_loop(0, (n + rb - 1) // rb, run_block, 0)

    @pl.when(e == N_EXPERTS - 1)
    def _():
        cw = 512
        for c0 in range(0, D_MODEL, cw):
            o_ref[:, c0:c0 + cw] = lax.dot_general(
                dall_ref[...], xs_ref[0:nslot, c0:c0 + cw], (((0,), (0,)), ((), ())),
                preferred_element_type=F32)


def _experts(h2, dest, wts, meta, w_g, w_u, w_d, tm):
    n, d = h2.shape
    nt = n // tm
    rb = 128
    nslot = -(-(2 * tm + N_EXPERTS * (BF16_ROWS - 1)) // 256) * 256
    grid_spec = pltpu.PrefetchScalarGridSpec(
        num_scalar_prefetch=1,
        grid=(nt, N_EXPERTS),
        in_specs=[_resident((tm, d), lambda t, e, m: (t, 0)),
                  pl.BlockSpec((1, 8, tm), lambda t, e, m: (t, 0, 0)),
                  pl.BlockSpec((1, 8, tm), lambda t, e, m: (t, 0, 0)),
                  pl.BlockSpec((1, d, D_EXPERT), lambda t, e, m: (e, 0, 0)),
                  pl.BlockSpec((1, d, D_EXPERT), lambda t, e, m: (e, 0, 0)),
                  pl.BlockSpec((1, D_EXPERT, d), lambda t, e, m: (e, 0, 0))],
        out_specs=pl.BlockSpec((tm, d), lambda t, e, m: (t, 0)),
        scratch_shapes=[pltpu.VMEM((nslot, tm), BF16),
                        pltpu.VMEM((nslot + rb, d), BF16),
                        pltpu.VMEM((nslot + rb, LANES), F32)])
    return pl.pallas_call(
        functools.partial(_experts_kernel, tm=tm, nslot=nslot, rb=rb),
        grid_spec=grid_spec,
        out_shape=jax.ShapeDtypeStruct((n, d), F32),
        compiler_params=_cparams(("arbitrary", "arbitrary")),
        name="moe_experts",
    )(meta, h2, dest, wts, w_g, w_u, w_d)


def _moe(h2, wrt, rbias, w_g, w_u, w_d, tm):
    dest, wts, meta = _router(h2, wrt, rbias, tm)
    return _experts(h2, dest, wts, meta[:, :, 0], w_g, w_u, w_d, tm)


def _final_kernel(x_ref, r_ref, gate_ref, g_ref, y_ref):
    y_ref[0] = _rms(x_ref[0] + gate_ref[0] * r_ref[0], g_ref[...])


def _final(x, resid, mod, row_mod, g, tm):
    nb, t, d = x.shape
    tile = pl.BlockSpec((1, tm, d), lambda b, i: (b, i, 0))
    return pl.pallas_call(
        _final_kernel,
        grid=(nb, t // tm),
        in_specs=[tile, tile, _mod_spec(row_mod, tm, 5, 2), pl.BlockSpec((1, d), lambda b, i: (0, 0))],
        out_specs=tile,
        out_shape=jax.ShapeDtypeStruct((nb, t, d), F32),
        compiler_params=_cparams(("arbitrary", "arbitrary")),
        name="final_norm",
    )(x, resid, mod, g.reshape(1, d))


def _reorder_w_in(w):
    d = w.shape[0]
    parts = [w[:, 0:6144], w[:, 8272:12368], w[:, 7168:8192], w[:, 6144:7168], w[:, 8192:8272]]
    used = sum(p.shape[1] for p in parts)
    parts.append(jnp.zeros((d, N_COLS - used), w.dtype))
    return jnp.concatenate(parts, axis=1).astype(BF16)


def _pick_tile(t, pref):
    tm = min(t, pref)
    assert t % tm == 0
    return tm


def kernel(x_prompt, x_sample, c_prompt, c_sample, cache_k, cache_v, cache_kidx, state_conv, norm1_g, norm2_g, w_mod, b_mod, w_in, conv_dw_w, conv_dw_b, conv_ln_g, conv_ln_b, w_conv_out, w_attn_out, w_out, router_g, router_g_b, router_e, router_e_b, w_gate, w_up, w_down, final_g):
    depth = w_in.shape[0]
    bp, seq, d = x_prompt.shape
    bs, dseq, _ = x_sample.shape
    past = cache_k.shape[2]
    n_keep = min(seq, past)
    dkv = N_KV_HEADS * HEAD_DIM
    kb = 256

    nc = bp + bs
    c_all = jnp.concatenate([c_prompt, c_sample, jnp.zeros((-nc % 8, d), F32)], axis=0)
    mod_all = _modulation(c_all, w_mod, b_mod)

    xp = x_prompt
    xs = x_sample.reshape(1, bs * dseq, d)
    res_p = res_s = None
    prev_mp = prev_ms = None
    outs = [[] for _ in range(8)]
    zero_buf = jnp.zeros((bp, HALO, d), F32)
    s_all = past + dseq
    s_pad = -(-s_all // kb) * kb

    for l in range(depth):
        mod_p = mod_all[l, :bp][:, None, :]
        mod_s = jnp.repeat(mod_all[l, bp:nc], dseq, axis=0)[None]
        wz = _reorder_w_in(w_in[l])
        w_pw = w_conv_out[l].astype(BF16)
        w_ao = w_attn_out[l].astype(BF16)
        w_o = w_out[l].astype(BF16)
        w_dw = jnp.concatenate([conv_dw_w[l], jnp.zeros((HALO - CONV_W, d), F32)], axis=0)
        wrt = jnp.concatenate([router_g[l].T, router_e[l].T,
                               jnp.zeros((LANES - N_GROUPS - N_EXPERTS, d), F32)], axis=0).astype(BF16)
        rbias = jnp.broadcast_to(jnp.concatenate(
            [router_g_b[l], router_e_b[l], jnp.zeros((LANES - N_GROUPS - N_EXPERTS,), F32)])[:, None],
            (LANES, LANES))
        wg, wu, wd = w_gate[l].astype(BF16), w_up[l].astype(BF16), w_down[l].astype(BF16)

        zp, xp = _inproj(xp, res_p, mod_p, prev_mp, False, norm1_g[l], wz, _pick_tile(seq, 512))
        yc_p, tail_p = _conv_branch(zp, zero_buf, w_dw, conv_dw_b[l], conv_ln_g[l], conv_ln_b[l],
                                    w_pw, _pick_tile(seq, 256))
        k_p = zp[:, :, C_K:C_K + dkv]
        v_p = zp[:, :, C_V:C_V + dkv]
        ki_p = zp[:, :, C_KIWI:C_KIWI + IDX_DIM]
        o_p = _attention(zp, k_p.astype(BF16), v_p.astype(BF16),
                         jnp.swapaxes(ki_p.astype(BF16), 1, 2), _pick_tile(seq, 128), kb, True, seq)
        xm_p, h2_p = _post(o_p, yc_p, zp, xp, mod_p, False, norm2_g[l], w_ao, w_o, _pick_tile(seq, 256))
        res_p = _moe(h2_p.reshape(bp * seq, d), wrt, rbias, wg, wu, wd,
                     _pick_tile(bp * seq, 1024)).reshape(bp, seq, d)
        xp, prev_mp = xm_p, mod_p

        zs, xs = _inproj(xs, res_s, mod_s, prev_ms, True, norm1_g[l], wz, bs * dseq)
        zs_b = zs.reshape(bs, dseq, N_COLS)
        init_s = jnp.concatenate([jnp.zeros((bs, HALO - (CONV_W - 1), d), F32), state_conv[l]], axis=1)
        yc_s, tail_s = _conv_branch(zs_b, init_s, w_dw, conv_dw_b[l], conv_ln_g[l], conv_ln_b[l],
                                    w_pw, dseq)
        k_s = zs_b[:, :, C_K:C_K + dkv]
        v_s = zs_b[:, :, C_V:C_V + dkv]
        ki_s = zs_b[:, :, C_KIWI:C_KIWI + IDX_DIM]
        padk = jnp.zeros((bs, s_pad - s_all, dkv), BF16)
        k_all = jnp.concatenate([cache_k[l].reshape(bs, past, dkv).astype(BF16), k_s.astype(BF16), padk], axis=1)
        v_all = jnp.concatenate([cache_v[l].reshape(bs, past, dkv).astype(BF16), v_s.astype(BF16), padk], axis=1)
        ki_all = jnp.concatenate([cache_kidx[l].astype(BF16), ki_s.astype(BF16),
                                  jnp.zeros((bs, s_pad - s_all, IDX_DIM), BF16)], axis=1)
        o_s = _attention(zs_b, k_all, v_all, jnp.swapaxes(ki_all, 1, 2), dseq, kb, False, s_all)
        xm_s, h2_s = _post(o_s.reshape(1, bs * dseq, d), yc_s.reshape(1, bs * dseq, d), zs, xs,
                           mod_s, True, norm2_g[l], w_ao, w_o, bs * dseq)
        res_s = _moe(h2_s.reshape(bs * dseq, d), wrt, rbias, wg, wu, wd,
                     bs * dseq).reshape(1, bs * dseq, d)
        xs, prev_ms = xm_s, mod_s

        outs[0].append(k_p[:, seq - n_keep:].reshape(bp, n_keep, N_KV_HEADS, HEAD_DIM))
        outs[1].append(v_p[:, seq - n_keep:].reshape(bp, n_keep, N_KV_HEADS, HEAD_DIM))
        outs[2].append(ki_p[:, seq - n_keep:])
        outs[3].append(tail_p[:, HALO - (CONV_W - 1):])
        outs[4].append(k_s.reshape(bs, dseq, N_KV_HEADS, HEAD_DIM))
        outs[5].append(v_s.reshape(bs, dseq, N_KV_HEADS, HEAD_DIM))
        outs[6].append(ki_s)
        outs[7].append(tail_s[:, HALO - (CONV_W - 1):])

    y_p = _final(xp, res_p, prev_mp, False, final_g, _pick_tile(seq, 512))
    y_s = _final(xs, res_s, prev_ms, True, final_g, bs * dseq).reshape(bs, dseq, d)
    return (y_p, y_s) + tuple(jnp.stack(o) for o in outs)
```

```python
import functools

import jax
import jax.numpy as jnp
from jax import lax
from jax.experimental import pallas as pl
from jax.experimental.pallas import tpu as pltpu

F32 = jnp.float32
BF16 = jnp.bfloat16
I32 = jnp.int32

D_MODEL = 2048
CHUNK = 64
CHUNK_SHIFT = 6
CONV_W = 31
N_HEADS = 16
N_KV_HEADS = 4
HEAD_DIM = 128
N_IDX_HEADS = 16
IDX_DIM = 64
TOPK_MAX = 256
N_GROUPS = 4
EXPERTS_PER_GROUP = 8
N_EXPERTS = N_GROUPS * EXPERTS_PER_GROUP
D_EXPERT = 512
EPS = 1e-6

LANES = 128
BF16_ROWS = 16
VMEM_LIMIT = 60 * 1024 * 1024

C_GA, C_GB, C_Q, C_GTA, C_GTB = 0, 2048, 4096, 6144, 8192
C_QI, C_K, C_V, C_KIWI = 10240, 11264, 11776, 12288
N_COLS = 12800
HALO = 32

LOG2_E = 1.4426950408889634
NEG_BIG = -1e30
INT_MIN = -(2 ** 31)
NEG_INF_KEY = INT_MIN + 0x7FFFFF


def _cparams(sem):
    return pltpu.CompilerParams(dimension_semantics=sem, vmem_limit_bytes=VMEM_LIMIT)


def _sigmoid(x):
    return 1.0 / (1.0 + jnp.exp(-x))


def _silu(x):
    return x * _sigmoid(x)


def _resident(shape, index_map):
    return pl.BlockSpec(shape, index_map, pipeline_mode=pl.Buffered(1))


def _mod_kernel(c_ref, w_ref, b_ref, o_ref):
    s = _silu(c_ref[...]).astype(BF16)
    o_ref[0] = jnp.dot(s, w_ref[0].astype(BF16), preferred_element_type=F32) + b_ref[0]


def _modulation(c_all, w_mod, b_mod):
    depth, d, n6 = w_mod.shape
    rows = c_all.shape[0]
    tn = 1024
    return pl.pallas_call(
        _mod_kernel,
        grid=(depth, n6 // tn),
        in_specs=[pl.BlockSpec((rows, d), lambda l, j: (0, 0)),
                  pl.BlockSpec((1, d, tn), lambda l, j: (l, 0, j)),
                  pl.BlockSpec((1, 1, tn), lambda l, j: (l, 0, j))],
        out_specs=pl.BlockSpec((1, rows, tn), lambda l, j: (l, 0, j)),
        out_shape=jax.ShapeDtypeStruct((depth, rows, n6), F32),
        compiler_params=_cparams(("arbitrary", "arbitrary")),
        name="modulation",
    )(c_all, w_mod, b_mod.reshape(depth, 1, n6))


def _mod_spec(row_mod, tm, k, grid_rank):
    if grid_rank == 3:
        if row_mod:
            return pl.BlockSpec((1, tm, D_MODEL), lambda b, i, n: (b, i, k))
        return pl.BlockSpec((1, 1, D_MODEL), lambda b, i, n: (b, 0, k))
    if row_mod:
        return pl.BlockSpec((1, tm, D_MODEL), lambda b, i: (b, i, k))
    return pl.BlockSpec((1, 1, D_MODEL), lambda b, i: (b, 0, k))


def _rms(x, g):
    ms = jnp.mean(x * x, axis=-1, keepdims=True)
    return x * lax.rsqrt(ms + EPS) * g


def _inproj_kernel(*refs, has_resid):
    if has_resid:
        x_ref, r_ref, gate_ref, sh_ref, sc_ref, g_ref, w_ref, z_ref, xo_ref, h_ref = refs
    else:
        x_ref, sh_ref, sc_ref, g_ref, w_ref, z_ref, h_ref = refs

    @pl.when(pl.program_id(2) == 0)
    def _():
        x = x_ref[0]
        if has_resid:
            x = x + gate_ref[0] * r_ref[0]
            xo_ref[0] = x
        h = _rms(x, g_ref[...]) * (1.0 + sc_ref[0]) + sh_ref[0]
        h_ref[...] = h.astype(BF16)

    z_ref[0] = jnp.dot(h_ref[...], w_ref[...], preferred_element_type=F32)


def _inproj(x, resid, mod, prev_mod, row_mod, g, w, tm):
    nb, t, d = x.shape
    tn = 1280
    has_resid = resid is not None
    xspec = pl.BlockSpec((1, tm, d), lambda b, i, n: (b, i, 0))
    in_specs = [xspec]
    args = [x]
    if has_resid:
        in_specs += [xspec, _mod_spec(row_mod, tm, 5, 3)]
        args += [resid, prev_mod]
    in_specs += [_mod_spec(row_mod, tm, 0, 3), _mod_spec(row_mod, tm, 1, 3),
                 pl.BlockSpec((1, d), lambda b, i, n: (0, 0)),
                 pl.BlockSpec((d, tn), lambda b, i, n: (0, n))]
    args += [mod, mod, g.reshape(1, d), w]
    zspec = pl.BlockSpec((1, tm, tn), lambda b, i, n: (b, i, n))
    zshape = jax.ShapeDtypeStruct((nb, t, N_COLS), F32)
    if has_resid:
        out_specs, out_shape = (zspec, xspec), (zshape, jax.ShapeDtypeStruct((nb, t, d), F32))
    else:
        out_specs, out_shape = zspec, zshape
    out = pl.pallas_call(
        functools.partial(_inproj_kernel, has_resid=has_resid),
        grid=(nb, t // tm, N_COLS // tn),
        in_specs=in_specs, out_specs=out_specs, out_shape=out_shape,
        scratch_shapes=[pltpu.VMEM((tm, d), BF16)],
        compiler_params=_cparams(("arbitrary", "arbitrary", "arbitrary")),
        name="inproj",
    )(*args)
    return out if has_resid else (out, x)


def _conv_kernel(a_ref, b_ref, ha_ref, hb_ref, init_ref, wdw_ref, bdw_ref, lng_ref, lnb_ref,
                 wpw_ref, y_ref, tail_ref, full_ref, cv_ref, *, tm):
    i = pl.program_id(1)
    full_ref[HALO:HALO + tm, :] = a_ref[0] * _sigmoid(b_ref[0])

    @pl.when(i == 0)
    def _():
        full_ref[0:HALO, :] = init_ref[0]

    @pl.when(i > 0)
    def _():
        full_ref[0:HALO, :] = ha_ref[0] * _sigmoid(hb_ref[0])

    tail_ref[0] = full_ref[tm:tm + HALO, :]

    cc = 512
    rb = min(tm, 64)
    off = HALO - (CONV_W - 1)
    for c0 in range(0, D_MODEL, cc):
        for r0 in range(0, tm, rb):
            acc = jnp.zeros((rb, cc), F32)
            for j in range(CONV_W):
                acc = acc + wdw_ref[j:j + 1, c0:c0 + cc] * full_ref[r0 + off + j:r0 + off + j + rb, c0:c0 + cc]
            cv_ref[r0:r0 + rb, c0:c0 + cc] = acc + bdw_ref[:, c0:c0 + cc]

    y = cv_ref[...]
    mu = jnp.mean(y, axis=-1, keepdims=True)
    yc = y - mu
    var = jnp.mean(yc * yc, axis=-1, keepdims=True)
    yn = yc * lax.rsqrt(var + EPS) * lng_ref[...] + lnb_ref[...]
    y_ref[0] = jnp.dot(_silu(yn).astype(BF16), wpw_ref[...], preferred_element_type=F32)


def _conv_branch(z, init_buf, w_dw, b_dw, ln_g, ln_b, w_pw, tm):
    nb, t, _ = z.shape
    d = D_MODEL
    hb = tm // HALO
    tile = lambda k: pl.BlockSpec((1, tm, d), lambda b, i: (b, i, k))
    halo = lambda k: pl.BlockSpec((1, HALO, d), lambda b, i: (b, jnp.maximum(i * hb - 1, 0), k))
    vec = pl.BlockSpec((1, d), lambda b, i: (0, 0))
    return pl.pallas_call(
        functools.partial(_conv_kernel, tm=tm),
        grid=(nb, t // tm),
        in_specs=[tile(C_GA // d), tile(C_GB // d), halo(C_GA // d), halo(C_GB // d),
                  pl.BlockSpec((1, HALO, d), lambda b, i: (b, 0, 0)),
                  pl.BlockSpec((HALO, d), lambda b, i: (0, 0)),
                  vec, vec, vec, _resident((d, d), lambda b, i: (0, 0))],
        out_specs=(pl.BlockSpec((1, tm, d), lambda b, i: (b, i, 0)),
                   pl.BlockSpec((1, HALO, d), lambda b, i: (b, 0, 0))),
        out_shape=(jax.ShapeDtypeStruct((nb, t, d), F32),
                   jax.ShapeDtypeStruct((nb, HALO, d), F32)),
        scratch_shapes=[pltpu.VMEM((tm + HALO, d), F32), pltpu.VMEM((tm, d), F32)],
        compiler_params=_cparams(("arbitrary", "arbitrary")),
        name="conv_branch",
    )(z, z, z, z, init_buf, w_dw, b_dw.reshape(1, d), ln_g.reshape(1, d), ln_b.reshape(1, d), w_pw)


def _attn_kernel(q_ref, qi_ref, kw_ref, k_ref, v_ref, ki_ref, o_ref,
                 keys_ref, qs_ref, qis_ref, wb_ref, m_ref, acc_ref,
                 *, tq, kb, s_pad, causal, s_valid, n_sel):
    i = pl.program_id(1)
    q0 = i * tq
    gq = N_HEADS // N_KV_HEADS
    nlb = kb // LANES
    if causal:
        nkb = (q0 + tq + kb - 1) // kb
    else:
        nkb = s_pad // kb

    for h in range(N_HEADS):
        g, j = divmod(h, gq)
        qs_ref[g, j * tq:(j + 1) * tq, :] = q_ref[0, :, h * HEAD_DIM:(h + 1) * HEAD_DIM].astype(BF16)
    kw = kw_ref[0]
    for h in range(N_IDX_HEADS):
        qis_ref[h * tq:(h + 1) * tq, :] = (
            qi_ref[0, :, h * IDX_DIM:(h + 1) * IDX_DIM] * (IDX_DIM ** -0.5)).astype(BF16)
        wcol = kw[:, IDX_DIM + h:IDX_DIM + h + 1] * (N_IDX_HEADS ** -0.5)
        wb_ref[h] = jnp.broadcast_to(wcol, (tq, LANES))

    rows = q0 + lax.broadcasted_iota(I32, (tq, 1), 0)
    if causal:
        limit = ((rows >> CHUNK_SHIFT) + 1) << CHUNK_SHIFT
    else:
        limit = jnp.full((tq, 1), s_valid, I32)

    sub = 256

    def score_block(b, carry):
        for u in range(kb // sub):
            c0 = pl.multiple_of(b * kb + u * sub, sub)
            d = lax.dot_general(qis_ref[...], ki_ref[0, pl.ds(c0, sub), :], (((1,), (1,)), ((), ())),
                                preferred_element_type=F32)
            acc = jnp.zeros((tq, sub), F32)
            for h in range(N_IDX_HEADS):
                w = jnp.concatenate([wb_ref[h]] * (sub // LANES), axis=1)
                acc = acc + w * jnp.maximum(d[h * tq:(h + 1) * tq], 0.0)
            pos = c0 + lax.broadcasted_iota(I32, (1, sub), 1)
            acc = jnp.where(pos < limit, acc, -jnp.inf)
            bits = pltpu.bitcast(acc, I32)
            keys_ref[:, pl.ds(c0, sub)] = bits ^ ((bits >> 31) & 0x7FFFFFFF)
        return carry

    lax.fori_loop(0, nkb, score_block, 0)

    def count_ge(thr):
        def body(b, c):
            blk = keys_ref[:, pl.ds(pl.multiple_of(b * kb, kb), kb)]
            hit = jnp.where(blk >= thr, 1.0, 0.0)
            for u in range(nlb):
                c = c + hit[:, u * LANES:(u + 1) * LANES]
            return c
        c = lax.fori_loop(0, nkb, body, jnp.zeros((tq, LANES), F32))
        return jnp.sum(c, axis=1, keepdims=True)

    def bit_cond(st):
        it, _, settled = st
        return jnp.logical_and(it < 32, jnp.min(settled) < 0.5)

    def bit_step(st):
        it, ans, settled = st
        cand = ans + (jnp.int32(1) << (31 - it))
        cnt = count_ge(cand)
        ans = jnp.where(settled > 0.5, ans, jnp.where(cnt >= float(n_sel), cand, ans))
        settled = jnp.where(cnt == float(n_sel), 1.0, settled)
        return it + 1, ans, settled

    _, kth, settled = lax.while_loop(
        bit_cond, bit_step,
        (jnp.int32(0), jnp.full((tq, 1), INT_MIN, I32), jnp.zeros((tq, 1), F32)))
    thr = jnp.maximum(kth, NEG_INF_KEY + 1)

    @pl.when(jnp.min(settled) < 0.5)
    def _():
        excess = count_ge(thr) - float(n_sel)

        @pl.when(jnp.max(excess) > 0.0)
        def _():
            def count_eq(b, c):
                blk = keys_ref[:, pl.ds(pl.multiple_of(b * kb, kb), kb)]
                hit = jnp.where(blk == thr, 1.0, 0.0)
                for u in range(nlb):
                    c = c + hit[:, u * LANES:(u + 1) * LANES]
                return c
            n_eq = jnp.sum(lax.fori_loop(0, nkb, count_eq, jnp.zeros((tq, LANES), F32)),
                           axis=1, keepdims=True)
            keep = jnp.where(excess > 0.0, n_eq - excess, n_eq)
            tri = jnp.where(lax.broadcasted_iota(I32, (LANES, LANES), 0)
                            <= lax.broadcasted_iota(I32, (LANES, LANES), 1), 1.0, 0.0).astype(BF16)

            def drop_block(b, run):
                c0 = pl.multiple_of(b * LANES, LANES)
                blk = keys_ref[:, pl.ds(c0, LANES)]
                eq = blk == thr
                eqf = jnp.where(eq, 1.0, 0.0)
                rank = run + jnp.dot(eqf.astype(BF16), tri, preferred_element_type=F32) - 1.0
                keys_ref[:, pl.ds(c0, LANES)] = jnp.where(eq & (rank >= keep), INT_MIN, blk)
                return run + jnp.sum(eqf, axis=1, keepdims=True)

            lax.fori_loop(0, nkb * nlb, drop_block, jnp.zeros((tq, 1), F32))

    m_ref[...] = jnp.full(m_ref.shape, NEG_BIG, F32)
    acc_ref[...] = jnp.zeros(acc_ref.shape, F32)
    scale = (HEAD_DIM ** -0.5) * LOG2_E
    ones = jnp.ones((kb, HEAD_DIM), BF16)

    def attend_block(b, carry):
        c0 = pl.multiple_of(b * kb, kb)
        bias = jnp.where(keys_ref[:, pl.ds(c0, kb)] >= thr, 0.0, NEG_BIG)
        bias = jnp.concatenate([bias] * gq, axis=0)
        for g in range(N_KV_HEADS):
            kg = k_ref[0, pl.ds(c0, kb), g * HEAD_DIM:(g + 1) * HEAD_DIM]
            vg = v_ref[0, pl.ds(c0, kb), g * HEAD_DIM:(g + 1) * HEAD_DIM]
            s = lax.dot_general(qs_ref[g], kg, (((1,), (1,)), ((), ())),
                                preferred_element_type=F32) * scale + bias
            m_prev = m_ref[g]
            m_new = jnp.maximum(m_prev, jnp.max(s, axis=1, keepdims=True))
            alpha = jnp.exp2(m_prev - m_new)
            p = jnp.exp2(s - jnp.concatenate([m_new] * nlb, axis=1))
            pv = jnp.dot(p.astype(BF16), jnp.concatenate([vg, ones], axis=1),
                         preferred_element_type=F32)
            acc_ref[g] = jnp.concatenate([alpha, alpha], axis=1) * acc_ref[g] + pv
            m_ref[g] = m_new
        return carry

    lax.fori_loop(0, nkb, attend_block, 0)

    for h in range(N_HEADS):
        g, j = divmod(h, gq)
        o = acc_ref[g, j * tq:(j + 1) * tq, 0:HEAD_DIM] / acc_ref[g, j * tq:(j + 1) * tq, HEAD_DIM:2 * HEAD_DIM]
        o_ref[0, :, h * HEAD_DIM:(h + 1) * HEAD_DIM] = o.astype(BF16)


def _attention(z, kb16, vb16, ki16, tq, kb, causal, s_valid):
    nb, t, _ = z.shape
    s_pad = kb16.shape[1]
    assert s_pad % kb == 0
    dq = N_HEADS * HEAD_DIM
    dqi = N_IDX_HEADS * IDX_DIM
    dkv = N_KV_HEADS * HEAD_DIM
    gq = N_HEADS // N_KV_HEADS
    return pl.pallas_call(
        functools.partial(_attn_kernel, tq=tq, kb=kb, s_pad=s_pad, causal=causal, s_valid=s_valid,
                          n_sel=min(TOPK_MAX, s_valid // 4)),
        grid=(nb, t // tq),
        in_specs=[pl.BlockSpec((1, tq, dq), lambda b, i: (b, i, C_Q // dq)),
                  pl.BlockSpec((1, tq, dqi), lambda b, i: (b, i, C_QI // dqi)),
                  pl.BlockSpec((1, tq, LANES), lambda b, i: (b, i, C_KIWI // LANES)),
                  _resident((1, s_pad, dkv), lambda b, i: (b, 0, 0)),
                  _resident((1, s_pad, dkv), lambda b, i: (b, 0, 0)),
                  _resident((1, s_pad, IDX_DIM), lambda b, i: (b, 0, 0))],
        out_specs=pl.BlockSpec((1, tq, dq), lambda b, i: (b, i, 0)),
        out_shape=jax.ShapeDtypeStruct((nb, t, dq), BF16),
        scratch_shapes=[pltpu.VMEM((tq, s_pad), I32),
                        pltpu.VMEM((N_KV_HEADS, gq * tq, HEAD_DIM), BF16),
                        pltpu.VMEM((N_IDX_HEADS * tq, IDX_DIM), BF16),
                        pltpu.VMEM((N_IDX_HEADS, tq, LANES), F32),
                        pltpu.VMEM((N_KV_HEADS, gq * tq, LANES), F32),
                        pltpu.VMEM((N_KV_HEADS, gq * tq, 2 * HEAD_DIM), F32)],
        compiler_params=_cparams(("arbitrary", "arbitrary")),
        name="sparse_attention",
    )(z, z, z, kb16, vb16, ki16)


def _post_kernel(o_ref, yc_ref, ga_ref, gb_ref, x_ref, g1_ref, sh2_ref, sc2_ref, n2_ref,
                 wao_ref, wout_ref, xm_ref, h2_ref):
    ya = jnp.dot(o_ref[0], wao_ref[...], preferred_element_type=F32)
    m = _sigmoid(ga_ref[0]) * yc_ref[0] + _sigmoid(gb_ref[0]) * ya
    mix = jnp.dot(m.astype(BF16), wout_ref[...], preferred_element_type=F32)
    x = x_ref[0] + g1_ref[0] * mix
    xm_ref[0] = x
    h2_ref[0] = (_rms(x, n2_ref[...]) * (1.0 + sc2_ref[0]) + sh2_ref[0]).astype(BF16)


def _post(o, y_conv, z, x, mod, row_mod, n2_g, w_ao, w_out, tm):
    nb, t, d = x.shape
    tile = lambda k: pl.BlockSpec((1, tm, d), lambda b, i: (b, i, k))
    return pl.pallas_call(
        _post_kernel,
        grid=(nb, t // tm),
        in_specs=[tile(0), tile(0), tile(C_GTA // d), tile(C_GTB // d), tile(0),
                  _mod_spec(row_mod, tm, 2, 2), _mod_spec(row_mod, tm, 3, 2),
                  _mod_spec(row_mod, tm, 4, 2),
                  pl.BlockSpec((1, d), lambda b, i: (0, 0)),
                  _resident((d, d), lambda b, i: (0, 0)),
                  _resident((d, d), lambda b, i: (0, 0))],
        out_specs=(tile(0), tile(0)),
        out_shape=(jax.ShapeDtypeStruct((nb, t, d), F32), jax.ShapeDtypeStruct((nb, t, d), BF16)),
        compiler_params=_cparams(("arbitrary", "arbitrary")),
        name="merge_out_proj",
    )(o, y_conv, z, z, x, mod, mod, mod, n2_g.reshape(1, d), w_ao, w_out)


def _router_kernel(h_ref, wrt_ref, rb_ref, dest_ref, wts_ref, meta_ref, *, tm):
    lt = lax.dot_general(wrt_ref[...], h_ref[...], (((1,), (1,)), ((), ())),
                         preferred_element_type=F32) + rb_ref[...][:, 0:1]
    row = lambda r: lt[r:r + 1, :]
    best, gi = row(0), jnp.zeros((1, tm), I32)
    for k in range(1, N_GROUPS):
        upd = row(k) > best
        best = jnp.where(upd, row(k), best)
        gi = jnp.where(upd, k, gi)
    sg = jnp.zeros((1, tm), F32)
    for k in range(N_GROUPS):
        sg = sg + jnp.exp(row(k) - best)
    p_group = 1.0 / sg

    le = []
    for j in range(EXPERTS_PER_GROUP):
        v = row(N_GROUPS + j)
        for g in range(1, N_GROUPS):
            v = jnp.where(gi == g, row(N_GROUPS + g * EXPERTS_PER_GROUP + j), v)
        le.append(v)
    b1, i1 = le[0], jnp.zeros((1, tm), I32)
    for j in range(1, EXPERTS_PER_GROUP):
        upd = le[j] > b1
        b1 = jnp.where(upd, le[j], b1)
        i1 = jnp.where(upd, j, i1)
    b2, i2 = jnp.full((1, tm), -jnp.inf, F32), jnp.zeros((1, tm), I32)
    for j in range(EXPERTS_PER_GROUP):
        upd = jnp.where(i1 == j, -jnp.inf, le[j]) > b2
        b2 = jnp.where(upd, le[j], b2)
        i2 = jnp.where(upd, j, i2)
    p2 = jnp.exp(b2 - b1)
    w0 = p_group / (1.0 + p2)
    w1 = p_group * p2 / (1.0 + p2)
    e0 = gi * EXPERTS_PER_GROUP + i1
    e1 = gi * EXPERTS_PER_GROUP + i2

    eio = lax.broadcasted_iota(I32, (N_EXPERTS, tm), 0)
    hit0 = jnp.where(eio == e0, 1.0, 0.0)
    hit1 = jnp.where(eio == e1, 1.0, 0.0)
    et = hit0 + hit1
    counts = jnp.sum(et, axis=1, keepdims=True)
    units = jnp.floor((counts + (BF16_ROWS - 1)) * (1.0 / BF16_ROWS))
    lower = jnp.where(lax.broadcasted_iota(I32, (N_EXPERTS, N_EXPERTS), 1)
                      < lax.broadcasted_iota(I32, (N_EXPERTS, N_EXPERTS), 0), 1.0, 0.0)
    starts = jnp.dot(lower.astype(BF16), jnp.broadcast_to(units, (N_EXPERTS, LANES)).astype(BF16),
                     preferred_element_type=F32) * float(BF16_ROWS)
    before = jnp.where(lax.broadcasted_iota(I32, (tm, tm), 0)
                       < lax.broadcasted_iota(I32, (tm, tm), 1), 1.0, 0.0).astype(BF16)
    ranks = jnp.dot(et.astype(BF16), before, preferred_element_type=F32)
    slot = starts[:, 0:1] + ranks
    d0 = jnp.sum(hit0 * slot, axis=0, keepdims=True)
    d1 = jnp.sum(hit1 * slot, axis=0, keepdims=True)
    dest_ref[0] = jnp.zeros((8, tm), I32)
    dest_ref[0, 0:1, :] = d0.astype(I32)
    dest_ref[0, 1:2, :] = d1.astype(I32)
    wts_ref[0] = jnp.zeros((8, tm), F32)
    wts_ref[0, 0:1, :] = w0
    wts_ref[0, 1:2, :] = w1
    meta_ref[0, 0:N_EXPERTS, :] = starts.astype(I32)
    meta_ref[0, N_EXPERTS:2 * N_EXPERTS, :] = jnp.broadcast_to(counts, (N_EXPERTS, LANES)).astype(I32)


def _router(h2, wrt, rbias, tm):
    n, d = h2.shape
    nt = n // tm
    return pl.pallas_call(
        functools.partial(_router_kernel, tm=tm),
        grid=(nt,),
        in_specs=[pl.BlockSpec((tm, d), lambda i: (i, 0)),
                  pl.BlockSpec((LANES, d), lambda i: (0, 0)),
                  pl.BlockSpec((LANES, LANES), lambda i: (0, 0))],
        out_specs=(pl.BlockSpec((1, 8, tm), lambda i: (i, 0, 0)),
                   pl.BlockSpec((1, 8, tm), lambda i: (i, 0, 0)),
                   pl.BlockSpec((1, 2 * N_EXPERTS, LANES), lambda i: (i, 0, 0))),
        out_shape=(jax.ShapeDtypeStruct((nt, 8, tm), I32),
                   jax.ShapeDtypeStruct((nt, 8, tm), F32),
                   jax.ShapeDtypeStruct((nt, 2 * N_EXPERTS, LANES), I32)),
        compiler_params=_cparams(("arbitrary",)),
        name="moe_router",
    )(h2, wrt, rbias)


def _experts_kernel(meta_ref, h_ref, dest_ref, wts_ref, wg_ref, wu_ref, wd_ref, o_ref,
                    dall_ref, xs_ref, ws_ref, *, tm, nslot, rb):
    t = pl.program_id(0)
    e = pl.program_id(1)
    sb = 256

    @pl.when(e == 0)
    def _():
        d0 = dest_ref[0, 0:1, :]
        d1 = dest_ref[0, 1:2, :]
        w0 = wts_ref[0, 0:1, :]
        w1 = wts_ref[0, 1:2, :]
        for s0 in range(0, nslot, sb):
            sio = s0 + lax.broadcasted_iota(I32, (sb, 1), 0)
            m0 = sio == d0
            m1 = sio == d1
            dd = (jnp.where(m0, 1.0, 0.0) + jnp.where(m1, 1.0, 0.0)).astype(BF16)
            dall_ref[s0:s0 + sb, :] = dd
            wcol = jnp.sum(jnp.where(m0, w0, 0.0) + jnp.where(m1, w1, 0.0), axis=1, keepdims=True)
            ws_ref[s0:s0 + sb, :] = jnp.broadcast_to(wcol, (sb, LANES))
            xs_ref[s0:s0 + sb, :] = jnp.dot(dd, h_ref[...], preferred_element_type=F32).astype(BF16)
        xs_ref[nslot:nslot + rb, :] = jnp.zeros((rb, D_MODEL), BF16)
        ws_ref[nslot:nslot + rb, :] = jnp.zeros((rb, LANES), F32)

    start = meta_ref[t, e]
    n = meta_ref[t, N_EXPERTS + e]

    def run_block(j, carry):
        r0 = pl.multiple_of(start + j * rb, BF16_ROWS)
        xe = xs_ref[pl.ds(r0, rb), :]
        g = jnp.dot(xe, wg_ref[0], preferred_element_type=F32)
        u = jnp.dot(xe, wu_ref[0], preferred_element_type=F32)
        y = jnp.dot((_silu(g) * u).astype(BF16), wd_ref[0], preferred_element_type=F32)
        y = y * jnp.concatenate([ws_ref[pl.ds(r0, rb), :]] * (D_MODEL // LANES), axis=1)
        valid = lax.broadcasted_iota(I32, (rb, 1), 0) < n - j * rb
        xs_ref[pl.ds(r0, rb), :] = jnp.where(valid, y.astype(BF16), xe)
        return carry

    lax.fori_loop(0, (n + rb - 1) // rb, run_block, 0)

    @pl.when(e == N_EXPERTS - 1)
    def _():
        cw = 512
        for c0 in range(0, D_MODEL, cw):
            o_ref[:, c0:c0 + cw] = lax.dot_general(
                dall_ref[...], xs_ref[0:nslot, c0:c0 + cw], (((0,), (0,)), ((), ())),
                preferred_element_type=F32)


def _experts(h2, dest, wts, meta, w_g, w_u, w_d, tm):
    n, d = h2.shape
    nt = n // tm
    rb = 128
    nslot = -(-(2 * tm + N_EXPERTS * (BF16_ROWS - 1)) // 256) * 256
    grid_spec = pltpu.PrefetchScalarGridSpec(
        num_scalar_prefetch=1,
        grid=(nt, N_EXPERTS),
        in_specs=[_resident((tm, d), lambda t, e, m: (t, 0)),
                  pl.BlockSpec((1, 8, tm), lambda t, e, m: (t, 0, 0)),
                  pl.BlockSpec((1, 8, tm), lambda t, e, m: (t, 0, 0)),
                  pl.BlockSpec((1, d, D_EXPERT), lambda t, e, m: (e, 0, 0)),
                  pl.BlockSpec((1, d, D_EXPERT), lambda t, e, m: (e, 0, 0)),
                  pl.BlockSpec((1, D_EXPERT, d), lambda t, e, m: (e, 0, 0))],
        out_specs=pl.BlockSpec((tm, d), lambda t, e, m: (t, 0)),
        scratch_shapes=[pltpu.VMEM((nslot, tm), BF16),
                        pltpu.VMEM((nslot + rb, d), BF16),
                        pltpu.VMEM((nslot + rb, LANES), F32)])
    return pl.pallas_call(
        functools.partial(_experts_kernel, tm=tm, nslot=nslot, rb=rb),
        grid_spec=grid_spec,
        out_shape=jax.ShapeDtypeStruct((n, d), F32),
        compiler_params=_cparams(("arbitrary", "arbitrary")),
        name="moe_experts",
    )(meta, h2, dest, wts, w_g, w_u, w_d)


def _moe(h2, wrt, rbias, w_g, w_u, w_d, tm):
    dest, wts, meta = _router(h2, wrt, rbias, tm)
    return _experts(h2, dest, wts, meta[:, :, 0], w_g, w_u, w_d, tm)


def _final_kernel(x_ref, r_ref, gate_ref, g_ref, y_ref):
    y_ref[0] = _rms(x_ref[0] + gate_ref[0] * r_ref[0], g_ref[...])


def _final(x, resid, mod, row_mod, g, tm):
    nb, t, d = x.shape
    tile = pl.BlockSpec((1, tm, d), lambda b, i: (b, i, 0))
    return pl.pallas_call(
        _final_kernel,
        grid=(nb, t // tm),
        in_specs=[tile, tile, _mod_spec(row_mod, tm, 5, 2), pl.BlockSpec((1, d), lambda b, i: (0, 0))],
        out_specs=tile,
        out_shape=jax.ShapeDtypeStruct((nb, t, d), F32),
        compiler_params=_cparams(("arbitrary", "arbitrary")),
        name="final_norm",
    )(x, resid, mod, g.reshape(1, d))


def _reorder_w_in(w):
    d = w.shape[0]
    parts = [w[:, 0:6144], w[:, 8272:12368], w[:, 7168:8192], w[:, 6144:7168], w[:, 8192:8272]]
    used = sum(p.shape[1] for p in parts)
    parts.append(jnp.zeros((d, N_COLS - used), w.dtype))
    return jnp.concatenate(parts, axis=1).astype(BF16)


def _pick_tile(t, pref):
    tm = min(t, pref)
    assert t % tm == 0
    return tm


def kernel(x_prompt, x_sample, c_prompt, c_sample, cache_k, cache_v, cache_kidx, state_conv, norm1_g, norm2_g, w_mod, b_mod, w_in, conv_dw_w, conv_dw_b, conv_ln_g, conv_ln_b, w_conv_out, w_attn_out, w_out, router_g, router_g_b, router_e, router_e_b, w_gate, w_up, w_down, final_g):
    depth = w_in.shape[0]
    bp, seq, d = x_prompt.shape
    bs, dseq, _ = x_sample.shape
    past = cache_k.shape[2]
    n_keep = min(seq, past)
    dkv = N_KV_HEADS * HEAD_DIM
    kb = 512

    nc = bp + bs
    c_all = jnp.concatenate([c_prompt, c_sample, jnp.zeros((-nc % 8, d), F32)], axis=0)
    mod_all = _modulation(c_all, w_mod, b_mod)

    xp = x_prompt
    xs = x_sample.reshape(1, bs * dseq, d)
    res_p = res_s = None
    prev_mp = prev_ms = None
    outs = [[] for _ in range(8)]
    zero_buf = jnp.zeros((bp, HALO, d), F32)
    s_all = past + dseq
    s_pad = -(-s_all // kb) * kb

    for l in range(depth):
        mod_p = mod_all[l, :bp][:, None, :]
        mod_s = jnp.repeat(mod_all[l, bp:nc], dseq, axis=0)[None]
        wz = _reorder_w_in(w_in[l])
        w_pw = w_conv_out[l].astype(BF16)
        w_ao = w_attn_out[l].astype(BF16)
        w_o = w_out[l].astype(BF16)
        w_dw = jnp.concatenate([conv_dw_w[l], jnp.zeros((HALO - CONV_W, d), F32)], axis=0)
        wrt = jnp.concatenate([router_g[l].T, router_e[l].T,
                               jnp.zeros((LANES - N_GROUPS - N_EXPERTS, d), F32)], axis=0).astype(BF16)
        rbias = jnp.broadcast_to(jnp.concatenate(
            [router_g_b[l], router_e_b[l], jnp.zeros((LANES - N_GROUPS - N_EXPERTS,), F32)])[:, None],
            (LANES, LANES))
        wg, wu, wd = w_gate[l].astype(BF16), w_up[l].astype(BF16), w_down[l].astype(BF16)

        zp, xp = _inproj(xp, res_p, mod_p, prev_mp, False, norm1_g[l], wz, _pick_tile(seq, 512))
        yc_p, tail_p = _conv_branch(zp, zero_buf, w_dw, conv_dw_b[l], conv_ln_g[l], conv_ln_b[l],
                                    w_pw, _pick_tile(seq, 256))
        k_p = zp[:, :, C_K:C_K + dkv]
        v_p = zp[:, :, C_V:C_V + dkv]
        ki_p = zp[:, :, C_KIWI:C_KIWI + IDX_DIM]
        o_p = _attention(zp, k_p.astype(BF16), v_p.astype(BF16), ki_p.astype(BF16),
                         _pick_tile(seq, 128), kb, True, seq)
        xm_p, h2_p = _post(o_p, yc_p, zp, xp, mod_p, False, norm2_g[l], w_ao, w_o, _pick_tile(seq, 256))
        res_p = _moe(h2_p.reshape(bp * seq, d), wrt, rbias, wg, wu, wd,
                     _pick_tile(bp * seq, 1024)).reshape(bp, seq, d)
        xp, prev_mp = xm_p, mod_p

        zs, xs = _inproj(xs, res_s, mod_s, prev_ms, True, norm1_g[l], wz, bs * dseq)
        zs_b = zs.reshape(bs, dseq, N_COLS)
        init_s = jnp.concatenate([jnp.zeros((bs, HALO - (CONV_W - 1), d), F32), state_conv[l]], axis=1)
        yc_s, tail_s = _conv_branch(zs_b, init_s, w_dw, conv_dw_b[l], conv_ln_g[l], conv_ln_b[l],
                                    w_pw, dseq)
        k_s = zs_b[:, :, C_K:C_K + dkv]
        v_s = zs_b[:, :, C_V:C_V + dkv]
        ki_s = zs_b[:, :, C_KIWI:C_KIWI + IDX_DIM]
        padk = jnp.zeros((bs, s_pad - s_all, dkv), BF16)
        k_all = jnp.concatenate([cache_k[l].reshape(bs, past, dkv).astype(BF16), k_s.astype(BF16), padk], axis=1)
        v_all = jnp.concatenate([cache_v[l].reshape(bs, past, dkv).astype(BF16), v_s.astype(BF16), padk], axis=1)
        ki_all = jnp.concatenate([cache_kidx[l].astype(BF16), ki_s.astype(BF16),
                                  jnp.zeros((bs, s_pad - s_all, IDX_DIM), BF16)], axis=1)
        o_s = _attention(zs_b, k_all, v_all, ki_all, dseq, kb, False, s_all)
        xm_s, h2_s = _post(o_s.reshape(1, bs * dseq, d), yc_s.reshape(1, bs * dseq, d), zs, xs,
                           mod_s, True, norm2_g[l], w_ao, w_o, bs * dseq)
        res_s = _moe(h2_s.reshape(bs * dseq, d), wrt, rbias, wg, wu, wd,
                     bs * dseq).reshape(1, bs * dseq, d)
        xs, prev_ms = xm_s, mod_s

        outs[0].append(k_p[:, seq - n_keep:].reshape(bp, n_keep, N_KV_HEADS, HEAD_DIM))
        outs[1].append(v_p[:, seq - n_keep:].reshape(bp, n_keep, N_KV_HEADS, HEAD_DIM))
        outs[2].append(ki_p[:, seq - n_keep:])
        outs[3].append(tail_p[:, HALO - (CONV_W - 1):])
        outs[4].append(k_s.reshape(bs, dseq, N_KV_HEADS, HEAD_DIM))
        outs[5].append(v_s.reshape(bs, dseq, N_KV_HEADS, HEAD_DIM))
        outs[6].append(ki_s)
        outs[7].append(tail_s[:, HALO - (CONV_W - 1):])

    y_p = _final(xp, res_p, prev_mp, False, final_g, _pick_tile(seq, 512))
    y_s = _final(xs, res_s, prev_ms, True, final_g, bs * dseq).reshape(bs, dseq, d)
    return (y_p, y_s) + tuple(jnp.stack(o) for o in outs)
```

```python
import functools

import jax
import jax.numpy as jnp
from jax import lax
from jax.experimental import pallas as pl
from jax.experimental.pallas import tpu as pltpu

F32 = jnp.float32
BF16 = jnp.bfloat16
I32 = jnp.int32

D_MODEL = 2048
CHUNK = 64
CHUNK_SHIFT = 6
CONV_W = 31
N_HEADS = 16
N_KV_HEADS = 4
HEAD_DIM = 128
N_IDX_HEADS = 16
IDX_DIM = 64
TOPK_MAX = 256
N_GROUPS = 4
EXPERTS_PER_GROUP = 8
N_EXPERTS = N_GROUPS * EXPERTS_PER_GROUP
D_EXPERT = 512
EPS = 1e-6

LANES = 128
SUBLANES = 8
BF16_ROWS = 16
VMEM_LIMIT = 60 * 1024 * 1024

C_GA, C_GB, C_Q, C_GTA, C_GTB, C_QI = 0, 2048, 4096, 6144, 8192, 10240
N_COLS_B = 11520
C_K, C_V, C_KIWI = 0, 512, 1024
N_COLS_KV = 1280
PROJ_TN = 1280
HALO = 32

LOG2_E = 1.4426950408889634
NEG_BIG = -1e30
INT_MIN = -(2 ** 31)
NEG_INF_KEY = INT_MIN + 0x7FFFFF


def _cparams(sem):
    return pltpu.CompilerParams(dimension_semantics=sem, vmem_limit_bytes=VMEM_LIMIT)


def _sigmoid(x):
    return 1.0 / (1.0 + jnp.exp(-x))


def _silu(x):
    return x * _sigmoid(x)


def _resident(shape, index_map):
    return pl.BlockSpec(shape, index_map, pipeline_mode=pl.Buffered(1))


def _mod_kernel(c_ref, w_ref, b_ref, o_ref):
    s = _silu(c_ref[...]).astype(BF16)
    o_ref[0] = jnp.dot(s, w_ref[0].astype(BF16), preferred_element_type=F32) + b_ref[0]


def _modulation(c_all, w_mod, b_mod):
    depth, d, n6 = w_mod.shape
    rows = c_all.shape[0]
    tn = 1024
    return pl.pallas_call(
        _mod_kernel,
        grid=(depth, n6 // tn),
        in_specs=[pl.BlockSpec((rows, d), lambda l, j: (0, 0)),
                  pl.BlockSpec((1, d, tn), lambda l, j: (l, 0, j)),
                  pl.BlockSpec((1, 1, tn), lambda l, j: (l, 0, j))],
        out_specs=pl.BlockSpec((1, rows, tn), lambda l, j: (l, 0, j)),
        out_shape=jax.ShapeDtypeStruct((depth, rows, n6), F32),
        compiler_params=_cparams(("arbitrary", "arbitrary")),
        name="modulation",
    )(c_all, w_mod, b_mod.reshape(depth, 1, n6))


def _mod_spec(row_mod, tm, k):
    if row_mod:
        return pl.BlockSpec((1, tm, D_MODEL), lambda b, i: (b, i, k))
    return pl.BlockSpec((1, 1, D_MODEL), lambda b, i: (b, 0, k))


def _rms(x, g):
    ms = jnp.mean(x * x, axis=-1, keepdims=True)
    return x * lax.rsqrt(ms + EPS) * g


def _prenorm_kernel(*refs, has_resid):
    if has_resid:
        x_ref, r_ref, gate_ref, sh_ref, sc_ref, g_ref, xo_ref, h_ref = refs
        x = x_ref[0] + gate_ref[0] * r_ref[0]
        xo_ref[0] = x
    else:
        x_ref, sh_ref, sc_ref, g_ref, h_ref = refs
        x = x_ref[0]
    h_ref[0] = (_rms(x, g_ref[...]) * (1.0 + sc_ref[0]) + sh_ref[0]).astype(BF16)


def _prenorm(x, resid, mod, prev_mod, row_mod, g, tm):
    nb, t, d = x.shape
    has_resid = resid is not None
    xspec = pl.BlockSpec((1, tm, d), lambda b, i: (b, i, 0))
    in_specs, args = [xspec], [x]
    if has_resid:
        in_specs += [xspec, _mod_spec(row_mod, tm, 5)]
        args += [resid, prev_mod]
    in_specs += [_mod_spec(row_mod, tm, 0), _mod_spec(row_mod, tm, 1),
                 pl.BlockSpec((1, d), lambda b, i: (0, 0))]
    args += [mod, mod, g.reshape(1, d)]
    hshape = jax.ShapeDtypeStruct((nb, t, d), BF16)
    if has_resid:
        out_specs, out_shape = (xspec, xspec), (jax.ShapeDtypeStruct((nb, t, d), F32), hshape)
    else:
        out_specs, out_shape = xspec, hshape
    out = pl.pallas_call(
        functools.partial(_prenorm_kernel, has_resid=has_resid),
        grid=(nb, t // tm),
        in_specs=in_specs, out_specs=out_specs, out_shape=out_shape,
        compiler_params=_cparams(("arbitrary", "arbitrary")),
        name="prenorm",
    )(*args)
    return (out[1], out[0]) if has_resid else (out, x)


def _proj_kernel(h_ref, w_ref, o_ref):
    o_ref[0] = jnp.dot(h_ref[0], w_ref[...], preferred_element_type=F32).astype(o_ref.dtype)


def _proj(h, w, out_dtype, tm):
    nb, t, d = h.shape
    n = w.shape[1]
    tn = PROJ_TN
    return pl.pallas_call(
        _proj_kernel,
        grid=(n // tn, nb, t // tm),
        in_specs=[pl.BlockSpec((1, tm, d), lambda j, b, i: (b, i, 0)),
                  pl.BlockSpec((d, tn), lambda j, b, i: (0, j))],
        out_specs=pl.BlockSpec((1, tm, tn), lambda j, b, i: (b, i, j)),
        out_shape=jax.ShapeDtypeStruct((nb, t, n), out_dtype),
        compiler_params=_cparams(("arbitrary", "arbitrary", "arbitrary")),
        name="inproj",
    )(h, w)


def _conv_kernel(a_ref, b_ref, ha_ref, hb_ref, init_ref, wdw_ref, bdw_ref, lng_ref, lnb_ref,
                 wpw_ref, y_ref, tail_ref, full_ref, cv_ref, *, tm):
    i = pl.program_id(1)
    full_ref[HALO:HALO + tm, :] = a_ref[0].astype(F32) * _sigmoid(b_ref[0].astype(F32))

    @pl.when(i == 0)
    def _():
        full_ref[0:HALO, :] = init_ref[0]

    @pl.when(i > 0)
    def _():
        full_ref[0:HALO, :] = ha_ref[0].astype(F32) * _sigmoid(hb_ref[0].astype(F32))

    tail_ref[0] = full_ref[tm:tm + HALO, :]

    cc = 256
    rb = min(tm, 64)
    off = HALO - (CONV_W - 1)
    for c0 in range(0, D_MODEL, cc):
        for r0 in range(0, tm, rb):
            acc = jnp.zeros((rb, cc), F32)
            for s in range(SUBLANES):
                taps = [j for j in range(CONV_W) if (off + j) % SUBLANES == s]
                span = max((off + j) // SUBLANES for j in taps) * SUBLANES + rb
                win = full_ref[r0 + s:r0 + s + span, c0:c0 + cc]
                for j in taps:
                    a0 = (off + j) // SUBLANES * SUBLANES
                    wj = jnp.concatenate([wdw_ref[j, :, c0:c0 + cc]] * (rb // SUBLANES), axis=0)
                    acc = acc + wj * win[a0:a0 + rb]
            cv_ref[r0:r0 + rb, c0:c0 + cc] = acc + bdw_ref[:, c0:c0 + cc]

    y = cv_ref[...]
    mu = jnp.mean(y, axis=-1, keepdims=True)
    yc = y - mu
    var = jnp.mean(yc * yc, axis=-1, keepdims=True)
    yn = yc * lax.rsqrt(var + EPS) * lng_ref[...] + lnb_ref[...]
    y_ref[0] = jnp.dot(_silu(yn).astype(BF16), wpw_ref[...], preferred_element_type=F32)


def _conv_branch(zb, init_buf, w_dw, b_dw, ln_g, ln_b, w_pw, tm):
    nb, t, _ = zb.shape
    d = D_MODEL
    hb = tm // HALO
    tile = lambda k: pl.BlockSpec((1, tm, d), lambda b, i: (b, i, k))
    halo = lambda k: pl.BlockSpec((1, HALO, d), lambda b, i: (b, jnp.maximum(i * hb - 1, 0), k))
    vec = pl.BlockSpec((1, d), lambda b, i: (0, 0))
    return pl.pallas_call(
        functools.partial(_conv_kernel, tm=tm),
        grid=(nb, t // tm),
        in_specs=[tile(C_GA // d), tile(C_GB // d), halo(C_GA // d), halo(C_GB // d),
                  pl.BlockSpec((1, HALO, d), lambda b, i: (b, 0, 0)),
                  pl.BlockSpec((CONV_W, SUBLANES, d), lambda b, i: (0, 0, 0)),
                  vec, vec, vec, _resident((d, d), lambda b, i: (0, 0))],
        out_specs=(pl.BlockSpec((1, tm, d), lambda b, i: (b, i, 0)),
                   pl.BlockSpec((1, HALO, d), lambda b, i: (b, 0, 0))),
        out_shape=(jax.ShapeDtypeStruct((nb, t, d), F32),
                   jax.ShapeDtypeStruct((nb, HALO, d), F32)),
        scratch_shapes=[pltpu.VMEM((tm + HALO, d), F32), pltpu.VMEM((tm, d), F32)],
        compiler_params=_cparams(("arbitrary", "arbitrary")),
        name="conv_branch",
    )(zb, zb, zb, zb, init_buf, w_dw, b_dw.reshape(1, d), ln_g.reshape(1, d), ln_b.reshape(1, d), w_pw)


def _attn_kernel(q_ref, qi_ref, kw_ref, k_ref, v_ref, ki_ref, o_ref,
                 keys_ref, qs_ref, qis_ref, wb_ref, m_ref, acc_ref,
                 *, tq, kb, s_pad, causal, s_valid, n_sel):
    i = pl.program_id(1)
    q0 = i * tq
    gq = N_HEADS // N_KV_HEADS
    nlb = kb // LANES
    if causal:
        nkb = (q0 + tq + kb - 1) // kb
    else:
        nkb = s_pad // kb

    for h in range(N_HEADS):
        g, j = divmod(h, gq)
        qs_ref[g, j * tq:(j + 1) * tq, :] = q_ref[0, :, h * HEAD_DIM:(h + 1) * HEAD_DIM].astype(BF16)
    kw = kw_ref[0]
    for h in range(N_IDX_HEADS):
        qis_ref[h * tq:(h + 1) * tq, :] = (
            qi_ref[0, :, h * IDX_DIM:(h + 1) * IDX_DIM] * (IDX_DIM ** -0.5)).astype(BF16)
        wcol = kw[:, IDX_DIM + h:IDX_DIM + h + 1] * (N_IDX_HEADS ** -0.5)
        wb_ref[h] = jnp.broadcast_to(wcol, (tq, LANES))

    rows = q0 + lax.broadcasted_iota(I32, (tq, 1), 0)
    if causal:
        limit = ((rows >> CHUNK_SHIFT) + 1) << CHUNK_SHIFT
    else:
        limit = jnp.full((tq, 1), s_valid, I32)

    sub = 256

    def score_block(b, carry):
        for u in range(kb // sub):
            c0 = pl.multiple_of(b * kb + u * sub, sub)
            d = lax.dot_general(qis_ref[...], ki_ref[0, pl.ds(c0, sub), :], (((1,), (1,)), ((), ())),
                                preferred_element_type=F32)
            acc = jnp.zeros((tq, sub), F32)
            for h in range(N_IDX_HEADS):
                w = jnp.concatenate([wb_ref[h]] * (sub // LANES), axis=1)
                acc = acc + w * jnp.maximum(d[h * tq:(h + 1) * tq], 0.0)
            pos = c0 + lax.broadcasted_iota(I32, (1, sub), 1)
            acc = jnp.where(pos < limit, acc, -jnp.inf)
            bits = pltpu.bitcast(acc, I32)
            keys_ref[:, pl.ds(c0, sub)] = bits ^ ((bits >> 31) & 0x7FFFFFFF)
        return carry

    lax.fori_loop(0, nkb, score_block, 0)

    def count_ge(thr):
        def body(b, c):
            blk = keys_ref[:, pl.ds(pl.multiple_of(b * kb, kb), kb)]
            hit = jnp.where(blk >= thr, 1.0, 0.0)
            for u in range(nlb):
                c = c + hit[:, u * LANES:(u + 1) * LANES]
            return c
        c = lax.fori_loop(0, nkb, body, jnp.zeros((tq, LANES), F32))
        return jnp.sum(c, axis=1, keepdims=True)

    def bit_cond(st):
        it, _, settled = st
        return jnp.logical_and(it < 32, jnp.min(settled) < 0.5)

    def bit_step(st):
        it, ans, settled = st
        cand = ans + (jnp.int32(1) << (31 - it))
        cnt = count_ge(cand)
        ans = jnp.where(settled > 0.5, ans, jnp.where(cnt >= float(n_sel), cand, ans))
        settled = jnp.where(cnt == float(n_sel), 1.0, settled)
        return it + 1, ans, settled

    _, kth, settled = lax.while_loop(
        bit_cond, bit_step,
        (jnp.int32(0), jnp.full((tq, 1), INT_MIN, I32), jnp.zeros((tq, 1), F32)))
    thr = jnp.maximum(kth, NEG_INF_KEY + 1)

    @pl.when(jnp.min(settled) < 0.5)
    def _():
        excess = count_ge(thr) - float(n_sel)

        @pl.when(jnp.max(excess) > 0.0)
        def _():
            def count_eq(b, c):
                blk = keys_ref[:, pl.ds(pl.multiple_of(b * kb, kb), kb)]
                hit = jnp.where(blk == thr, 1.0, 0.0)
                for u in range(nlb):
                    c = c + hit[:, u * LANES:(u + 1) * LANES]
                return c
            n_eq = jnp.sum(lax.fori_loop(0, nkb, count_eq, jnp.zeros((tq, LANES), F32)),
                           axis=1, keepdims=True)
            keep = jnp.where(excess > 0.0, n_eq - excess, n_eq)
            tri = jnp.where(lax.broadcasted_iota(I32, (LANES, LANES), 0)
                            <= lax.broadcasted_iota(I32, (LANES, LANES), 1), 1.0, 0.0).astype(BF16)

            def drop_block(b, run):
                c0 = pl.multiple_of(b * LANES, LANES)
                blk = keys_ref[:, pl.ds(c0, LANES)]
                eq = blk == thr
                eqf = jnp.where(eq, 1.0, 0.0)
                rank = run + jnp.dot(eqf.astype(BF16), tri, preferred_element_type=F32) - 1.0
                keys_ref[:, pl.ds(c0, LANES)] = jnp.where(eq & (rank >= keep), INT_MIN, blk)
                return run + jnp.sum(eqf, axis=1, keepdims=True)

            lax.fori_loop(0, nkb * nlb, drop_block, jnp.zeros((tq, 1), F32))

    m_ref[...] = jnp.full(m_ref.shape, NEG_BIG, F32)
    acc_ref[...] = jnp.zeros(acc_ref.shape, F32)
    scale = (HEAD_DIM ** -0.5) * LOG2_E
    ones = jnp.ones((kb, HEAD_DIM), BF16)

    def attend_block(b, carry):
        c0 = pl.multiple_of(b * kb, kb)
        bias = jnp.where(keys_ref[:, pl.ds(c0, kb)] >= thr, 0.0, NEG_BIG)
        bias = jnp.concatenate([bias] * gq, axis=0)
        for g in range(N_KV_HEADS):
            kg = k_ref[0, pl.ds(c0, kb), g * HEAD_DIM:(g + 1) * HEAD_DIM]
            vg = v_ref[0, pl.ds(c0, kb), g * HEAD_DIM:(g + 1) * HEAD_DIM]
            s = lax.dot_general(qs_ref[g], kg, (((1,), (1,)), ((), ())),
                                preferred_element_type=F32) * scale + bias
            m_prev = m_ref[g]
            m_new = jnp.maximum(m_prev, jnp.max(s, axis=1, keepdims=True))
            alpha = jnp.exp2(m_prev - m_new)
            p = jnp.exp2(s - jnp.concatenate([m_new] * nlb, axis=1))
            pv = jnp.dot(p.astype(BF16), jnp.concatenate([vg, ones], axis=1),
                         preferred_element_type=F32)
            acc_ref[g] = jnp.concatenate([alpha, alpha], axis=1) * acc_ref[g] + pv
            m_ref[g] = m_new
        return carry

    lax.fori_loop(0, nkb, attend_block, 0)

    for h in range(N_HEADS):
        g, j = divmod(h, gq)
        o = acc_ref[g, j * tq:(j + 1) * tq, 0:HEAD_DIM] / acc_ref[g, j * tq:(j + 1) * tq, HEAD_DIM:2 * HEAD_DIM]
        o_ref[0, :, h * HEAD_DIM:(h + 1) * HEAD_DIM] = o.astype(BF16)


def _attention(zb, zkv, kb16, vb16, ki16, tq, kb, causal, s_valid):
    nb, t, _ = zb.shape
    s_pad = kb16.shape[1]
    assert s_pad % kb == 0
    dq = N_HEADS * HEAD_DIM
    dqi = N_IDX_HEADS * IDX_DIM
    dkv = N_KV_HEADS * HEAD_DIM
    gq = N_HEADS // N_KV_HEADS
    return pl.pallas_call(
        functools.partial(_attn_kernel, tq=tq, kb=kb, s_pad=s_pad, causal=causal, s_valid=s_valid,
                          n_sel=min(TOPK_MAX, s_valid // 4)),
        grid=(nb, t // tq),
        in_specs=[pl.BlockSpec((1, tq, dq), lambda b, i: (b, i, C_Q // dq)),
                  pl.BlockSpec((1, tq, dqi), lambda b, i: (b, i, C_QI // dqi)),
                  pl.BlockSpec((1, tq, LANES), lambda b, i: (b, i, C_KIWI // LANES)),
                  _resident((1, s_pad, dkv), lambda b, i: (b, 0, 0)),
                  _resident((1, s_pad, dkv), lambda b, i: (b, 0, 0)),
                  _resident((1, s_pad, IDX_DIM), lambda b, i: (b, 0, 0))],
        out_specs=pl.BlockSpec((1, tq, dq), lambda b, i: (b, i, 0)),
        out_shape=jax.ShapeDtypeStruct((nb, t, dq), BF16),
        scratch_shapes=[pltpu.VMEM((tq, s_pad), I32),
                        pltpu.VMEM((N_KV_HEADS, gq * tq, HEAD_DIM), BF16),
                        pltpu.VMEM((N_IDX_HEADS * tq, IDX_DIM), BF16),
                        pltpu.VMEM((N_IDX_HEADS, tq, LANES), F32),
                        pltpu.VMEM((N_KV_HEADS, gq * tq, LANES), F32),
                        pltpu.VMEM((N_KV_HEADS, gq * tq, 2 * HEAD_DIM), F32)],
        compiler_params=_cparams(("arbitrary", "arbitrary")),
        name="sparse_attention",
    )(zb, zb, zkv, kb16, vb16, ki16)


def _post_kernel(o_ref, yc_ref, ga_ref, gb_ref, x_ref, g1_ref, sh2_ref, sc2_ref, n2_ref,
                 wao_ref, wout_ref, xm_ref, h2_ref):
    ya = jnp.dot(o_ref[0], wao_ref[...], preferred_element_type=F32)
    m = _sigmoid(ga_ref[0].astype(F32)) * yc_ref[0] + _sigmoid(gb_ref[0].astype(F32)) * ya
    mix = jnp.dot(m.astype(BF16), wout_ref[...], preferred_element_type=F32)
    x = x_ref[0] + g1_ref[0] * mix
    xm_ref[0] = x
    h2_ref[0] = (_rms(x, n2_ref[...]) * (1.0 + sc2_ref[0]) + sh2_ref[0]).astype(BF16)


def _post(o, y_conv, zb, x, mod, row_mod, n2_g, w_ao, w_out, tm):
    nb, t, d = x.shape
    tile = lambda k: pl.BlockSpec((1, tm, d), lambda b, i: (b, i, k))
    return pl.pallas_call(
        _post_kernel,
        grid=(nb, t // tm),
        in_specs=[tile(0), tile(0), tile(C_GTA // d), tile(C_GTB // d), tile(0),
                  _mod_spec(row_mod, tm, 2), _mod_spec(row_mod, tm, 3),
                  _mod_spec(row_mod, tm, 4),
                  pl.BlockSpec((1, d), lambda b, i: (0, 0)),
                  _resident((d, d), lambda b, i: (0, 0)),
                  _resident((d, d), lambda b, i: (0, 0))],
        out_specs=(tile(0), tile(0)),
        out_shape=(jax.ShapeDtypeStruct((nb, t, d), F32), jax.ShapeDtypeStruct((nb, t, d), BF16)),
        compiler_params=_cparams(("arbitrary", "arbitrary")),
        name="merge_out_proj",
    )(o, y_conv, zb, zb, x, mod, mod, mod, n2_g.reshape(1, d), w_ao, w_out)


def _router_kernel(h_ref, wrt_ref, rb_ref, dest_ref, wts_ref, meta_ref, *, tm):
    lt = lax.dot_general(wrt_ref[...], h_ref[...], (((1,), (1,)), ((), ())),
                         preferred_element_type=F32) + rb_ref[...][:, 0:1]
    row = lambda r: lt[r:r + 1, :]
    best, gi = row(0), jnp.zeros((1, tm), I32)
    for k in range(1, N_GROUPS):
        upd = row(k) > best
        best = jnp.where(upd, row(k), best)
        gi = jnp.where(upd, k, gi)
    sg = jnp.zeros((1, tm), F32)
    for k in range(N_GROUPS):
        sg = sg + jnp.exp(row(k) - best)
    p_group = 1.0 / sg

    le = []
    for j in range(EXPERTS_PER_GROUP):
        v = row(N_GROUPS + j)
        for g in range(1, N_GROUPS):
            v = jnp.where(gi == g, row(N_GROUPS + g * EXPERTS_PER_GROUP + j), v)
        le.append(v)
    b1, i1 = le[0], jnp.zeros((1, tm), I32)
    for j in range(1, EXPERTS_PER_GROUP):
        upd = le[j] > b1
        b1 = jnp.where(upd, le[j], b1)
        i1 = jnp.where(upd, j, i1)
    b2, i2 = jnp.full((1, tm), -jnp.inf, F32), jnp.zeros((1, tm), I32)
    for j in range(EXPERTS_PER_GROUP):
        upd = jnp.where(i1 == j, -jnp.inf, le[j]) > b2
        b2 = jnp.where(upd, le[j], b2)
        i2 = jnp.where(upd, j, i2)
    p2 = jnp.exp(b2 - b1)
    w0 = p_group / (1.0 + p2)
    w1 = p_group * p2 / (1.0 + p2)
    e0 = gi * EXPERTS_PER_GROUP + i1
    e1 = gi * EXPERTS_PER_GROUP + i2

    eio = lax.broadcasted_iota(I32, (N_EXPERTS, tm), 0)
    hit0 = jnp.where(eio == e0, 1.0, 0.0)
    hit1 = jnp.where(eio == e1, 1.0, 0.0)
    et = hit0 + hit1
    counts = jnp.sum(et, axis=1, keepdims=True)
    units = jnp.floor((counts + (BF16_ROWS - 1)) * (1.0 / BF16_ROWS))
    lower = jnp.where(lax.broadcasted_iota(I32, (N_EXPERTS, N_EXPERTS), 1)
                      < lax.broadcasted_iota(I32, (N_EXPERTS, N_EXPERTS), 0), 1.0, 0.0)
    starts = jnp.dot(lower.astype(BF16), jnp.broadcast_to(units, (N_EXPERTS, LANES)).astype(BF16),
                     preferred_element_type=F32) * float(BF16_ROWS)
    before = jnp.where(lax.broadcasted_iota(I32, (tm, tm), 0)
                       < lax.broadcasted_iota(I32, (tm, tm), 1), 1.0, 0.0).astype(BF16)
    ranks = jnp.dot(et.astype(BF16), before, preferred_element_type=F32)
    slot = starts[:, 0:1] + ranks
    d0 = jnp.sum(hit0 * slot, axis=0, keepdims=True)
    d1 = jnp.sum(hit1 * slot, axis=0, keepdims=True)
    dest_ref[0] = jnp.zeros((8, tm), I32)
    dest_ref[0, 0:1, :] = d0.astype(I32)
    dest_ref[0, 1:2, :] = d1.astype(I32)
    wts_ref[0] = jnp.zeros((8, tm), F32)
    wts_ref[0, 0:1, :] = w0
    wts_ref[0, 1:2, :] = w1
    meta_ref[0, 0:N_EXPERTS, :] = starts.astype(I32)
    meta_ref[0, N_EXPERTS:2 * N_EXPERTS, :] = jnp.broadcast_to(counts, (N_EXPERTS, LANES)).astype(I32)


def _router(h2, wrt, rbias, tm):
    n, d = h2.shape
    nt = n // tm
    return pl.pallas_call(
        functools.partial(_router_kernel, tm=tm),
        grid=(nt,),
        in_specs=[pl.BlockSpec((tm, d), lambda i: (i, 0)),
                  pl.BlockSpec((LANES, d), lambda i: (0, 0)),
                  pl.BlockSpec((LANES, LANES), lambda i: (0, 0))],
        out_specs=(pl.BlockSpec((1, 8, tm), lambda i: (i, 0, 0)),
                   pl.BlockSpec((1, 8, tm), lambda i: (i, 0, 0)),
                   pl.BlockSpec((1, 2 * N_EXPERTS, LANES), lambda i: (i, 0, 0))),
        out_shape=(jax.ShapeDtypeStruct((nt, 8, tm), I32),
                   jax.ShapeDtypeStruct((nt, 8, tm), F32),
                   jax.ShapeDtypeStruct((nt, 2 * N_EXPERTS, LANES), I32)),
        compiler_params=_cparams(("arbitrary",)),
        name="moe_router",
    )(h2, wrt, rbias)


def _experts_kernel(meta_ref, h_ref, dest_ref, wts_ref, wg_ref, wu_ref, wd_ref, o_ref,
                    dall_ref, xs_ref, ws_ref, *, tm, nslot, rb):
    t = pl.program_id(0)
    e = pl.program_id(1)
    sb = 256

    @pl.when(e == 0)
    def _():
        d0 = dest_ref[0, 0:1, :]
        d1 = dest_ref[0, 1:2, :]
        w0 = wts_ref[0, 0:1, :]
        w1 = wts_ref[0, 1:2, :]
        for s0 in range(0, nslot, sb):
            sio = s0 + lax.broadcasted_iota(I32, (sb, 1), 0)
            m0 = sio == d0
            m1 = sio == d1
            dd = (jnp.where(m0, 1.0, 0.0) + jnp.where(m1, 1.0, 0.0)).astype(BF16)
            dall_ref[s0:s0 + sb, :] = dd
            wcol = jnp.sum(jnp.where(m0, w0, 0.0) + jnp.where(m1, w1, 0.0), axis=1, keepdims=True)
            ws_ref[s0:s0 + sb, :] = jnp.broadcast_to(wcol, (sb, LANES))
            xs_ref[s0:s0 + sb, :] = jnp.dot(dd, h_ref[...], preferred_element_type=F32).astype(BF16)
        xs_ref[nslot:nslot + rb, :] = jnp.zeros((rb, D_MODEL), BF16)
        ws_ref[nslot:nslot + rb, :] = jnp.zeros((rb, LANES), F32)

    start = meta_ref[t, e]
    n = meta_ref[t, N_EXPERTS + e]

    def run_block(j, carry):
        r0 = pl.multiple_of(start + j * rb, BF16_ROWS)
        xe = xs_ref[pl.ds(r0, rb), :]
        g = jnp.dot(xe, wg_ref[0], preferred_element_type=F32)
        u = jnp.dot(xe, wu_ref[0], preferred_element_type=F32)
        y = jnp.dot((_silu(g) * u).astype(BF16), wd_ref[0], preferred_element_type=F32)
        y = y * jnp.concatenate([ws_ref[pl.ds(r0, rb), :]] * (D_MODEL // LANES), axis=1)
        valid = lax.broadcasted_iota(I32, (rb, 1), 0) < n - j * rb
        xs_ref[pl.ds(r0, rb), :] = jnp.where(valid, y.astype(BF16), xe)
        return carry

    lax.fori_loop(0, (n + rb - 1) // rb, run_block, 0)

    @pl.when(e == N_EXPERTS - 1)
    def _():
        cw = 512
        for c0 in range(0, D_MODEL, cw):
            o_ref[:, c0:c0 + cw] = lax.dot_general(
                dall_ref[...], xs_ref[0:nslot, c0:c0 + cw], (((0,), (0,)), ((), ())),
                preferred_element_type=F32)


def _experts(h2, dest, wts, meta, w_g, w_u, w_d, tm):
    n, d = h2.shape
    nt = n // tm
    rb = 128
    nslot = -(-(2 * tm + N_EXPERTS * (BF16_ROWS - 1)) // 256) * 256
    grid_spec = pltpu.PrefetchScalarGridSpec(
        num_scalar_prefetch=1,
        grid=(nt, N_EXPERTS),
        in_specs=[_resident((tm, d), lambda t, e, m: (t, 0)),
                  pl.BlockSpec((1, 8, tm), lambda t, e, m: (t, 0, 0)),
                  pl.BlockSpec((1, 8, tm), lambda t, e, m: (t, 0, 0)),
                  pl.BlockSpec((1, d, D_EXPERT), lambda t, e, m: (e, 0, 0)),
                  pl.BlockSpec((1, d, D_EXPERT), lambda t, e, m: (e, 0, 0)),
                  pl.BlockSpec((1, D_EXPERT, d), lambda t, e, m: (e, 0, 0))],
        out_specs=pl.BlockSpec((tm, d), lambda t, e, m: (t, 0)),
        scratch_shapes=[pltpu.VMEM((nslot, tm), BF16),
                        pltpu.VMEM((nslot + rb, d), BF16),
                        pltpu.VMEM((nslot + rb, LANES), F32)])
    return pl.pallas_call(
        functools.partial(_experts_kernel, tm=tm, nslot=nslot, rb=rb),
        grid_spec=grid_spec,
        out_shape=jax.ShapeDtypeStruct((n, d), F32),
        compiler_params=_cparams(("arbitrary", "arbitrary")),
        name="moe_experts",
    )(meta, h2, dest, wts, w_g, w_u, w_d)


def _moe(h2, wrt, rbias, w_g, w_u, w_d, tm):
    dest, wts, meta = _router(h2, wrt, rbias, tm)
    return _experts(h2, dest, wts, meta[:, :, 0], w_g, w_u, w_d, tm)


def _final_kernel(x_ref, r_ref, gate_ref, g_ref, y_ref):
    y_ref[0] = _rms(x_ref[0] + gate_ref[0] * r_ref[0], g_ref[...])


def _final(x, resid, mod, row_mod, g, tm):
    nb, t, d = x.shape
    tile = pl.BlockSpec((1, tm, d), lambda b, i: (b, i, 0))
    return pl.pallas_call(
        _final_kernel,
        grid=(nb, t // tm),
        in_specs=[tile, tile, _mod_spec(row_mod, tm, 5), pl.BlockSpec((1, d), lambda b, i: (0, 0))],
        out_specs=tile,
        out_shape=jax.ShapeDtypeStruct((nb, t, d), F32),
        compiler_params=_cparams(("arbitrary", "arbitrary")),
        name="final_norm",
    )(x, resid, mod, g.reshape(1, d))


def _split_w_in(w):
    d = w.shape[0]
    parts_b = [w[:, 0:6144], w[:, 8272:12368], w[:, 7168:8192]]
    parts_kv = [w[:, 6144:7168], w[:, 8192:8272]]
    pad = lambda parts, n: parts + [jnp.zeros((d, n - sum(p.shape[1] for p in parts)), w.dtype)]
    return (jnp.concatenate(pad(parts_b, N_COLS_B), axis=1).astype(BF16),
            jnp.concatenate(pad(parts_kv, N_COLS_KV), axis=1).astype(BF16))


def _pick_tile(t, pref):
    tm = min(t, pref)
    assert t % tm == 0
    return tm


def kernel(x_prompt, x_sample, c_prompt, c_sample, cache_k, cache_v, cache_kidx, state_conv, norm1_g, norm2_g, w_mod, b_mod, w_in, conv_dw_w, conv_dw_b, conv_ln_g, conv_ln_b, w_conv_out, w_attn_out, w_out, router_g, router_g_b, router_e, router_e_b, w_gate, w_up, w_down, final_g):
    depth = w_in.shape[0]
    bp, seq, d = x_prompt.shape
    bs, dseq, _ = x_sample.shape
    past = cache_k.shape[2]
    n_keep = min(seq, past)
    dkv = N_KV_HEADS * HEAD_DIM
    kb = 512

    nc = bp + bs
    c_all = jnp.concatenate([c_prompt, c_sample, jnp.zeros((-nc % 8, d), F32)], axis=0)
    mod_all = _modulation(c_all, w_mod, b_mod)

    xp = x_prompt
    xs = x_sample.reshape(1, bs * dseq, d)
    res_p = res_s = None
    prev_mp = prev_ms = None
    outs = [[] for _ in range(8)]
    zero_buf = jnp.zeros((bp, HALO, d), F32)
    s_all = past + dseq
    s_pad = -(-s_all // kb) * kb

    for l in range(depth):
        mod_p = mod_all[l, :bp][:, None, :]
        mod_s = jnp.repeat(mod_all[l, bp:nc], dseq, axis=0)[None]
        wz_b, wz_kv = _split_w_in(w_in[l])
        w_pw = w_conv_out[l].astype(BF16)
        w_ao = w_attn_out[l].astype(BF16)
        w_o = w_out[l].astype(BF16)
        w_dw = jnp.broadcast_to(conv_dw_w[l][:, None, :], (CONV_W, SUBLANES, d))
        wrt = jnp.concatenate([router_g[l].T, router_e[l].T,
                               jnp.zeros((LANES - N_GROUPS - N_EXPERTS, d), F32)], axis=0).astype(BF16)
        rbias = jnp.broadcast_to(jnp.concatenate(
            [router_g_b[l], router_e_b[l], jnp.zeros((LANES - N_GROUPS - N_EXPERTS,), F32)])[:, None],
            (LANES, LANES))
        wg, wu, wd = w_gate[l].astype(BF16), w_up[l].astype(BF16), w_down[l].astype(BF16)

        hp, xp = _prenorm(xp, res_p, mod_p, prev_mp, False, norm1_g[l], _pick_tile(seq, 512))
        zp = _proj(hp, wz_b, BF16, _pick_tile(seq, 1024))
        zkv_p = _proj(hp, wz_kv, F32, _pick_tile(seq, 1024))
        yc_p, tail_p = _conv_branch(zp, zero_buf, w_dw, conv_dw_b[l], conv_ln_g[l], conv_ln_b[l],
                                    w_pw, _pick_tile(seq, 256))
        k_p = zkv_p[:, :, C_K:C_K + dkv]
        v_p = zkv_p[:, :, C_V:C_V + dkv]
        ki_p = zkv_p[:, :, C_KIWI:C_KIWI + IDX_DIM]
        o_p = _attention(zp, zkv_p, k_p.astype(BF16), v_p.astype(BF16), ki_p.astype(BF16),
                         _pick_tile(seq, 128), kb, True, seq)
        xm_p, h2_p = _post(o_p, yc_p, zp, xp, mod_p, False, norm2_g[l], w_ao, w_o, _pick_tile(seq, 256))
        res_p = _moe(h2_p.reshape(bp * seq, d), wrt, rbias, wg, wu, wd,
                     _pick_tile(bp * seq, 1024)).reshape(bp, seq, d)
        xp, prev_mp = xm_p, mod_p

        hs, xs = _prenorm(xs, res_s, mod_s, prev_ms, True, norm1_g[l], bs * dseq)
        zs = _proj(hs, wz_b, BF16, bs * dseq)
        zs_b = zs.reshape(bs, dseq, N_COLS_B)
        zkv_s = _proj(hs, wz_kv, F32, bs * dseq).reshape(bs, dseq, N_COLS_KV)
        init_s = jnp.concatenate([jnp.zeros((bs, HALO - (CONV_W - 1), d), F32), state_conv[l]], axis=1)
        yc_s, tail_s = _conv_branch(zs_b, init_s, w_dw, conv_dw_b[l], conv_ln_g[l], conv_ln_b[l],
                                    w_pw, dseq)
        k_s = zkv_s[:, :, C_K:C_K + dkv]
        v_s = zkv_s[:, :, C_V:C_V + dkv]
        ki_s = zkv_s[:, :, C_KIWI:C_KIWI + IDX_DIM]
        padk = jnp.zeros((bs, s_pad - s_all, dkv), BF16)
        k_all = jnp.concatenate([cache_k[l].reshape(bs, past, dkv).astype(BF16), k_s.astype(BF16), padk], axis=1)
        v_all = jnp.concatenate([cache_v[l].reshape(bs, past, dkv).astype(BF16), v_s.astype(BF16), padk], axis=1)
        ki_all = jnp.concatenate([cache_kidx[l].astype(BF16), ki_s.astype(BF16),
                                  jnp.zeros((bs, s_pad - s_all, IDX_DIM), BF16)], axis=1)
        o_s = _attention(zs_b, zkv_s, k_all, v_all, ki_all, dseq, kb, False, s_all)
        xm_s, h2_s = _post(o_s.reshape(1, bs * dseq, d), yc_s.reshape(1, bs * dseq, d), zs, xs,
                           mod_s, True, norm2_g[l], w_ao, w_o, bs * dseq)
        res_s = _moe(h2_s.reshape(bs * dseq, d), wrt, rbias, wg, wu, wd,
                     bs * dseq).reshape(1, bs * dseq, d)
        xs, prev_ms = xm_s, mod_s

        outs[0].append(k_p[:, seq - n_keep:].reshape(bp, n_keep, N_KV_HEADS, HEAD_DIM))
        outs[1].append(v_p[:, seq - n_keep:].reshape(bp, n_keep, N_KV_HEADS, HEAD_DIM))
        outs[2].append(ki_p[:, seq - n_keep:])
        outs[3].append(tail_p[:, HALO - (CONV_W - 1):])
        outs[4].append(k_s.reshape(bs, dseq, N_KV_HEADS, HEAD_DIM))
        outs[5].append(v_s.reshape(bs, dseq, N_KV_HEADS, HEAD_DIM))
        outs[6].append(ki_s)
        outs[7].append(tail_s[:, HALO - (CONV_W - 1):])

    y_p = _final(xp, res_p, prev_mp, False, final_g, _pick_tile(seq, 512))
    y_s = _final(xs, res_s, prev_ms, True, final_g, bs * dseq).reshape(bs, dseq, d)
    return (y_p, y_s) + tuple(jnp.stack(o) for o in outs)
```

```python
import functools
import math

import jax
import jax.numpy as jnp
from jax import lax
from jax.experimental import pallas as pl
from jax.experimental.pallas import tpu as pltpu

F32 = jnp.float32
BF16 = jnp.bfloat16
I32 = jnp.int32

D_MODEL = 2048
CHUNK = 64
CHUNK_SHIFT = 6
CONV_W = 31
N_HEADS = 16
N_KV_HEADS = 4
HEAD_DIM = 128
N_IDX_HEADS = 16
IDX_DIM = 64
TOPK_MAX = 256
N_GROUPS = 4
EXPERTS_PER_GROUP = 8
N_EXPERTS = N_GROUPS * EXPERTS_PER_GROUP
D_EXPERT = 512
EPS = 1e-6

LANES = 128
SUBLANES = 8
BF16_ROWS = 16
VMEM_LIMIT = 60 * 1024 * 1024

C_GA, C_GB, C_Q, C_GTA, C_GTB, C_QI = 0, 2048, 4096, 6144, 8192, 10240
N_COLS_B = 11520
C_K, C_V, C_KIWI = 0, 512, 1024
N_COLS_KV = 1280
PROJ_TN = 1280
HALO = 32

LOG2_E = 1.4426950408889634
NEG_BIG = -1e30
INT_MIN = -(2 ** 31)
NEG_INF_KEY = INT_MIN + 0x7FFFFF


def _cparams(sem):
    return pltpu.CompilerParams(dimension_semantics=sem, vmem_limit_bytes=VMEM_LIMIT)


def _sigmoid(x):
    return 1.0 / (1.0 + jnp.exp(-x))


def _silu(x):
    return x * _sigmoid(x)


def _f2k(x):
    b = pltpu.bitcast(x, I32)
    return b ^ ((b >> 31) & 0x7FFFFFFF)


def _k2f(k):
    return pltpu.bitcast(k ^ ((k >> 31) & 0x7FFFFFFF), F32)


def _resident(shape, index_map):
    return pl.BlockSpec(shape, index_map, pipeline_mode=pl.Buffered(1))


def _mod_kernel(c_ref, w_ref, b_ref, o_ref):
    s = _silu(c_ref[...]).astype(BF16)
    o_ref[0] = jnp.dot(s, w_ref[0].astype(BF16), preferred_element_type=F32) + b_ref[0]


def _modulation(c_all, w_mod, b_mod):
    depth, d, n6 = w_mod.shape
    rows = c_all.shape[0]
    tn = 1024
    return pl.pallas_call(
        _mod_kernel,
        grid=(depth, n6 // tn),
        in_specs=[pl.BlockSpec((rows, d), lambda l, j: (0, 0)),
                  pl.BlockSpec((1, d, tn), lambda l, j: (l, 0, j)),
                  pl.BlockSpec((1, 1, tn), lambda l, j: (l, 0, j))],
        out_specs=pl.BlockSpec((1, rows, tn), lambda l, j: (l, 0, j)),
        out_shape=jax.ShapeDtypeStruct((depth, rows, n6), F32),
        compiler_params=_cparams(("arbitrary", "arbitrary")),
        name="modulation",
    )(c_all, w_mod, b_mod.reshape(depth, 1, n6))


def _mod_spec(row_mod, tm, k):
    if row_mod:
        return pl.BlockSpec((1, tm, D_MODEL), lambda b, i: (b, i, k))
    return pl.BlockSpec((1, 1, D_MODEL), lambda b, i: (b, 0, k))


def _rms(x, g):
    ms = jnp.mean(x * x, axis=-1, keepdims=True)
    return x * lax.rsqrt(ms + EPS) * g


def _prenorm_kernel(*refs, has_resid):
    if has_resid:
        x_ref, r_ref, gate_ref, sh_ref, sc_ref, g_ref, xo_ref, h_ref = refs
        x = x_ref[0] + gate_ref[0] * r_ref[0]
        xo_ref[0] = x
    else:
        x_ref, sh_ref, sc_ref, g_ref, h_ref = refs
        x = x_ref[0]
    h_ref[0] = (_rms(x, g_ref[...]) * (1.0 + sc_ref[0]) + sh_ref[0]).astype(BF16)


def _prenorm(x, resid, mod, prev_mod, row_mod, g, tm):
    nb, t, d = x.shape
    has_resid = resid is not None
    xspec = pl.BlockSpec((1, tm, d), lambda b, i: (b, i, 0))
    in_specs, args = [xspec], [x]
    if has_resid:
        in_specs += [xspec, _mod_spec(row_mod, tm, 5)]
        args += [resid, prev_mod]
    in_specs += [_mod_spec(row_mod, tm, 0), _mod_spec(row_mod, tm, 1),
                 pl.BlockSpec((1, d), lambda b, i: (0, 0))]
    args += [mod, mod, g.reshape(1, d)]
    hshape = jax.ShapeDtypeStruct((nb, t, d), BF16)
    if has_resid:
        out_specs, out_shape = (xspec, xspec), (jax.ShapeDtypeStruct((nb, t, d), F32), hshape)
    else:
        out_specs, out_shape = xspec, hshape
    out = pl.pallas_call(
        functools.partial(_prenorm_kernel, has_resid=has_resid),
        grid=(nb, t // tm),
        in_specs=in_specs, out_specs=out_specs, out_shape=out_shape,
        compiler_params=_cparams(("arbitrary", "arbitrary")),
        name="prenorm",
    )(*args)
    return (out[1], out[0]) if has_resid else (out, x)


def _proj_kernel(h_ref, w_ref, o_ref):
    o_ref[0] = jnp.dot(h_ref[0], w_ref[...], preferred_element_type=F32).astype(o_ref.dtype)


def _proj(h, w, out_dtype, tm):
    nb, t, d = h.shape
    n = w.shape[1]
    tn = PROJ_TN
    return pl.pallas_call(
        _proj_kernel,
        grid=(n // tn, nb, t // tm),
        in_specs=[pl.BlockSpec((1, tm, d), lambda j, b, i: (b, i, 0)),
                  pl.BlockSpec((d, tn), lambda j, b, i: (0, j))],
        out_specs=pl.BlockSpec((1, tm, tn), lambda j, b, i: (b, i, j)),
        out_shape=jax.ShapeDtypeStruct((nb, t, n), out_dtype),
        compiler_params=_cparams(("arbitrary", "arbitrary", "arbitrary")),
        name="inproj",
    )(h, w)


def _conv_kernel(a_ref, b_ref, ha_ref, hb_ref, init_ref, wdw_ref, bdw_ref, lng_ref, lnb_ref,
                 wpw_ref, y_ref, tail_ref, full_ref, cv_ref, *, tm):
    i = pl.program_id(1)
    full_ref[HALO:HALO + tm, :] = a_ref[0].astype(F32) * _sigmoid(b_ref[0].astype(F32))

    @pl.when(i == 0)
    def _():
        full_ref[0:HALO, :] = init_ref[0]

    @pl.when(i > 0)
    def _():
        full_ref[0:HALO, :] = ha_ref[0].astype(F32) * _sigmoid(hb_ref[0].astype(F32))

    tail_ref[0] = full_ref[tm:tm + HALO, :]

    cc = 256
    rb = min(tm, 64)
    off = HALO - (CONV_W - 1)
    for c0 in range(0, D_MODEL, cc):
        for r0 in range(0, tm, rb):
            acc = jnp.zeros((rb, cc), F32)
            for s in range(SUBLANES):
                taps = [j for j in range(CONV_W) if (off + j) % SUBLANES == s]
                span = max((off + j) // SUBLANES for j in taps) * SUBLANES + rb
                win = full_ref[r0 + s:r0 + s + span, c0:c0 + cc]
                for j in taps:
                    a0 = (off + j) // SUBLANES * SUBLANES
                    wj = jnp.concatenate([wdw_ref[j, :, c0:c0 + cc]] * (rb // SUBLANES), axis=0)
                    acc = acc + wj * win[a0:a0 + rb]
            cv_ref[r0:r0 + rb, c0:c0 + cc] = acc + bdw_ref[:, c0:c0 + cc]

    y = cv_ref[...]
    mu = jnp.mean(y, axis=-1, keepdims=True)
    yc = y - mu
    var = jnp.mean(yc * yc, axis=-1, keepdims=True)
    yn = yc * lax.rsqrt(var + EPS) * lng_ref[...] + lnb_ref[...]
    y_ref[0] = jnp.dot(_silu(yn).astype(BF16), wpw_ref[...], preferred_element_type=F32)


def _conv_branch(zb, init_buf, w_dw, b_dw, ln_g, ln_b, w_pw, tm):
    nb, t, _ = zb.shape
    d = D_MODEL
    hb = tm // HALO
    tile = lambda k: pl.BlockSpec((1, tm, d), lambda b, i: (b, i, k))
    halo = lambda k: pl.BlockSpec((1, HALO, d), lambda b, i: (b, jnp.maximum(i * hb - 1, 0), k))
    vec = pl.BlockSpec((1, d), lambda b, i: (0, 0))
    return pl.pallas_call(
        functools.partial(_conv_kernel, tm=tm),
        grid=(nb, t // tm),
        in_specs=[tile(C_GA // d), tile(C_GB // d), halo(C_GA // d), halo(C_GB // d),
                  pl.BlockSpec((1, HALO, d), lambda b, i: (b, 0, 0)),
                  pl.BlockSpec((CONV_W, SUBLANES, d), lambda b, i: (0, 0, 0)),
                  vec, vec, vec, _resident((d, d), lambda b, i: (0, 0))],
        out_specs=(pl.BlockSpec((1, tm, d), lambda b, i: (b, i, 0)),
                   pl.BlockSpec((1, HALO, d), lambda b, i: (b, 0, 0))),
        out_shape=(jax.ShapeDtypeStruct((nb, t, d), F32),
                   jax.ShapeDtypeStruct((nb, HALO, d), F32)),
        scratch_shapes=[pltpu.VMEM((tm + HALO, d), F32), pltpu.VMEM((tm, d), F32)],
        compiler_params=_cparams(("arbitrary", "arbitrary")),
        name="conv_branch",
    )(zb, zb, zb, zb, init_buf, w_dw, b_dw.reshape(1, d), ln_g.reshape(1, d), ln_b.reshape(1, d), w_pw)


def _attn_kernel(q_ref, qi_ref, kw_ref, k_ref, v_ref, ki_ref, o_ref,
                 keys_ref, qs_ref, qis_ref, wb_ref, m_ref, acc_ref, sc_ref,
                 *, tq, kb, s_pad, causal, s_valid, n_sel):
    i = pl.program_id(1)
    q0 = i * tq
    gq = N_HEADS // N_KV_HEADS
    nlb = kb // LANES
    kbs = 2 * kb
    if causal:
        nkb = (q0 + tq + kb - 1) // kb
        nkbs = (q0 + tq + kbs - 1) // kbs
    else:
        nkb = s_pad // kb
        nkbs = s_pad // kbs

    for h in range(N_HEADS):
        g, j = divmod(h, gq)
        qs_ref[g, j * tq:(j + 1) * tq, :] = q_ref[0, :, h * HEAD_DIM:(h + 1) * HEAD_DIM].astype(BF16)
    kw = kw_ref[0]
    for h in range(N_IDX_HEADS):
        qis_ref[h * tq:(h + 1) * tq, :] = (
            qi_ref[0, :, h * IDX_DIM:(h + 1) * IDX_DIM] * (IDX_DIM ** -0.5)).astype(BF16)
        wcol = kw[:, IDX_DIM + h:IDX_DIM + h + 1] * (N_IDX_HEADS ** -0.5)
        wb_ref[h] = jnp.broadcast_to(wcol, (tq, LANES))

    rows = q0 + lax.broadcasted_iota(I32, (tq, 1), 0)
    if causal:
        limit = ((rows >> CHUNK_SHIFT) + 1) << CHUNK_SHIFT
    else:
        limit = jnp.full((tq, 1), s_valid, I32)

    sub = 256

    def score_block(b, carry):
        m1, m2 = carry
        for u in range(kbs // sub):
            c0 = pl.multiple_of(b * kbs + u * sub, sub)
            d = lax.dot_general(qis_ref[...], ki_ref[0, pl.ds(c0, sub), :], (((1,), (1,)), ((), ())),
                                preferred_element_type=F32)
            acc = jnp.zeros((tq, sub), F32)
            for h in range(N_IDX_HEADS):
                w = jnp.concatenate([wb_ref[h]] * (sub // LANES), axis=1)
                acc = acc + w * jnp.maximum(d[h * tq:(h + 1) * tq], 0.0)
            pos = c0 + lax.broadcasted_iota(I32, (1, sub), 1)
            acc = jnp.where(pos < limit, acc, -jnp.inf)
            for v in range(sub // LANES):
                x = acc[:, v * LANES:(v + 1) * LANES]
                m2 = jnp.maximum(m2, jnp.minimum(m1, x))
                m1 = jnp.maximum(m1, x)
            keys_ref[:, pl.ds(c0, sub)] = _f2k(acc)
        return m1, m2

    neg = jnp.full((tq, LANES), -jnp.inf, F32)
    m1, m2 = lax.fori_loop(0, nkbs, score_block, (neg, neg))

    def count_ge(thr):
        def body(b, c):
            blk = keys_ref[:, pl.ds(pl.multiple_of(b * kb, kb), kb)]
            hit = jnp.where(blk >= thr, 1.0, 0.0)
            for u in range(nlb):
                c = c + hit[:, u * LANES:(u + 1) * LANES]
            return c
        c = lax.fori_loop(0, nkb, body, jnp.zeros((tq, LANES), F32))
        return jnp.sum(c, axis=1, keepdims=True)

    nsf = float(n_sel)
    log_n = math.log(n_sel)
    one = jnp.ones((tq, 1), F32)

    def live(lo, hi, settled):
        return jnp.where(settled < 0.5, jnp.where(hi > lo + 1, 1.0, 0.0), 0.0)

    def search_cond(st):
        it, lo, hi, settled = st[0], st[1], st[2], st[8]
        return jnp.logical_and(it < 100, jnp.max(live(lo, hi, settled)) > 0.5)

    def search_step(st):
        it, lo, hi, clo, chi, wlo, whi, last, settled, thr = st
        flo = (jnp.log(clo + 0.5) - log_n) * wlo
        fhi = (jnp.log(chi + 0.5) - log_n) * whi
        ck = _f2k((_k2f(lo) * (-fhi) + _k2f(hi) * flo) / (flo - fhi))
        mid = (lo & hi) + ((lo ^ hi) >> 1)
        inside = jnp.where(ck > lo, jnp.where(ck < hi, 1.0, 0.0), 0.0)
        take_mid = jnp.where(lax.rem(it, 3) == 2, 0.0, inside) < 0.5
        cand = jnp.where(take_mid, mid, ck)
        cnt = count_ge(cand)
        act = live(lo, hi, settled)
        up_lo = jnp.where(cnt >= nsf, act, 0.0) > 0.5
        up_hi = jnp.where(cnt >= nsf, 0.0, act) > 0.5
        hit = jnp.where(cnt == nsf, act, 0.0) > 0.5
        thr = jnp.where(hit, cand, thr)
        settled = jnp.where(hit, 1.0, settled)
        whi_n = jnp.where(up_lo, jnp.where(last == 1.0, whi * 0.5, whi), jnp.where(up_hi, 1.0, whi))
        wlo_n = jnp.where(up_hi, jnp.where(last == 2.0, wlo * 0.5, wlo), jnp.where(up_lo, 1.0, wlo))
        last = jnp.where(up_lo, 1.0, jnp.where(up_hi, 2.0, last))
        return (it + 1, jnp.where(up_lo, cand, lo), jnp.where(up_hi, cand, hi),
                jnp.where(up_lo, cnt, clo), jnp.where(up_hi, cnt, chi), wlo_n, whi_n, last, settled, thr)

    st = lax.while_loop(search_cond, search_step, (
        jnp.int32(0),
        _f2k(jnp.min(m2, axis=1, keepdims=True)),
        _f2k(jnp.max(m1, axis=1, keepdims=True)) + 1,
        2.0 * nsf * one, 0.0 * one, one, one, 0.0 * one,
        jnp.where(limit <= n_sel, 1.0, 0.0),
        jnp.full((tq, 1), NEG_INF_KEY + 1, I32)))
    settled = st[8]
    thr = jnp.maximum(jnp.where(settled > 0.5, st[9], st[1]), NEG_INF_KEY + 1)

    @pl.when(jnp.min(settled) < 0.5)
    def _():
        excess = count_ge(thr) - float(n_sel)

        @pl.when(jnp.max(excess) > 0.0)
        def _():
            def count_eq(b, c):
                blk = keys_ref[:, pl.ds(pl.multiple_of(b * kb, kb), kb)]
                hit = jnp.where(blk == thr, 1.0, 0.0)
                for u in range(nlb):
                    c = c + hit[:, u * LANES:(u + 1) * LANES]
                return c
            n_eq = jnp.sum(lax.fori_loop(0, nkb, count_eq, jnp.zeros((tq, LANES), F32)),
                           axis=1, keepdims=True)
            keep = jnp.where(excess > 0.0, n_eq - excess, n_eq)
            tri = jnp.where(lax.broadcasted_iota(I32, (LANES, LANES), 0)
                            <= lax.broadcasted_iota(I32, (LANES, LANES), 1), 1.0, 0.0).astype(BF16)

            def drop_block(b, run):
                c0 = pl.multiple_of(b * LANES, LANES)
                blk = keys_ref[:, pl.ds(c0, LANES)]
                eq = blk == thr
                eqf = jnp.where(eq, 1.0, 0.0)
                rank = run + jnp.dot(eqf.astype(BF16), tri, preferred_element_type=F32) - 1.0
                keys_ref[:, pl.ds(c0, LANES)] = jnp.where(eq & (rank >= keep), INT_MIN, blk)
                return run + jnp.sum(eqf, axis=1, keepdims=True)

            lax.fori_loop(0, nkb * nlb, drop_block, jnp.zeros((tq, 1), F32))

    m_ref[...] = jnp.full(m_ref.shape, NEG_BIG, F32)
    acc_ref[...] = jnp.zeros(acc_ref.shape, F32)
    scale = (HEAD_DIM ** -0.5) * LOG2_E
    ones = jnp.ones((kb, HEAD_DIM), BF16)

    def qk(g, c0):
        kg = k_ref[0, pl.ds(c0, kb), g * HEAD_DIM:(g + 1) * HEAD_DIM]
        return lax.dot_general(qs_ref[g], kg, (((1,), (1,)), ((), ())), preferred_element_type=F32)

    sc_ref[...] = qk(0, 0)

    def attend_block(b, carry):
        c0 = pl.multiple_of(b * kb, kb)
        c1 = pl.multiple_of(jnp.minimum(b + 1, nkb - 1) * kb, kb)
        bias = jnp.where(keys_ref[:, pl.ds(c0, kb)] >= thr, 0.0, NEG_BIG)
        bias = jnp.concatenate([bias] * gq, axis=0)
        raw = [sc_ref[...]] + [qk(g, c0) for g in range(1, N_KV_HEADS)]
        sc_ref[...] = qk(0, c1)
        for g in range(N_KV_HEADS):
            vg = v_ref[0, pl.ds(c0, kb), g * HEAD_DIM:(g + 1) * HEAD_DIM]
            s = raw[g] * scale + bias
            m_prev = m_ref[g]
            m_new = jnp.maximum(m_prev, jnp.max(s, axis=1, keepdims=True))
            alpha = jnp.exp2(m_prev - m_new)
            p = jnp.exp2(s - jnp.concatenate([m_new] * nlb, axis=1))
            pv = jnp.dot(p.astype(BF16), jnp.concatenate([vg, ones], axis=1),
                         preferred_element_type=F32)
            acc_ref[g] = jnp.concatenate([alpha, alpha], axis=1) * acc_ref[g] + pv
            m_ref[g] = m_new
        return carry

    lax.fori_loop(0, nkb, attend_block, 0)

    for h in range(N_HEADS):
        g, j = divmod(h, gq)
        o = acc_ref[g, j * tq:(j + 1) * tq, 0:HEAD_DIM] / acc_ref[g, j * tq:(j + 1) * tq, HEAD_DIM:2 * HEAD_DIM]
        o_ref[0, :, h * HEAD_DIM:(h + 1) * HEAD_DIM] = o.astype(BF16)


def _attention(zb, zkv, kb16, vb16, ki16, tq, kb, causal, s_valid):
    nb, t, _ = zb.shape
    s_pad = kb16.shape[1]
    assert s_pad % (2 * kb) == 0
    assert min(TOPK_MAX, s_valid // 4) <= 2 * LANES
    dq = N_HEADS * HEAD_DIM
    dqi = N_IDX_HEADS * IDX_DIM
    dkv = N_KV_HEADS * HEAD_DIM
    gq = N_HEADS // N_KV_HEADS
    return pl.pallas_call(
        functools.partial(_attn_kernel, tq=tq, kb=kb, s_pad=s_pad, causal=causal, s_valid=s_valid,
                          n_sel=min(TOPK_MAX, s_valid // 4)),
        grid=(nb, t // tq),
        in_specs=[pl.BlockSpec((1, tq, dq), lambda b, i: (b, i, C_Q // dq)),
                  pl.BlockSpec((1, tq, dqi), lambda b, i: (b, i, C_QI // dqi)),
                  pl.BlockSpec((1, tq, LANES), lambda b, i: (b, i, C_KIWI // LANES)),
                  _resident((1, s_pad, dkv), lambda b, i: (b, 0, 0)),
                  _resident((1, s_pad, dkv), lambda b, i: (b, 0, 0)),
                  _resident((1, s_pad, IDX_DIM), lambda b, i: (b, 0, 0))],
        out_specs=pl.BlockSpec((1, tq, dq), lambda b, i: (b, i, 0)),
        out_shape=jax.ShapeDtypeStruct((nb, t, dq), BF16),
        scratch_shapes=[pltpu.VMEM((tq, s_pad), I32),
                        pltpu.VMEM((N_KV_HEADS, gq * tq, HEAD_DIM), BF16),
                        pltpu.VMEM((N_IDX_HEADS * tq, IDX_DIM), BF16),
                        pltpu.VMEM((N_IDX_HEADS, tq, LANES), F32),
                        pltpu.VMEM((N_KV_HEADS, gq * tq, LANES), F32),
                        pltpu.VMEM((N_KV_HEADS, gq * tq, 2 * HEAD_DIM), F32),
                        pltpu.VMEM((gq * tq, kb), F32)],
        compiler_params=_cparams(("arbitrary", "arbitrary")),
        name="sparse_attention",
    )(zb, zb, zkv, kb16, vb16, ki16)


def _post_kernel(o_ref, yc_ref, ga_ref, gb_ref, x_ref, g1_ref, sh2_ref, sc2_ref, n2_ref,
                 wao_ref, wout_ref, xm_ref, h2_ref):
    ya = jnp.dot(o_ref[0], wao_ref[...], preferred_element_type=F32)
    m = _sigmoid(ga_ref[0].astype(F32)) * yc_ref[0] + _sigmoid(gb_ref[0].astype(F32)) * ya
    mix = jnp.dot(m.astype(BF16), wout_ref[...], preferred_element_type=F32)
    x = x_ref[0] + g1_ref[0] * mix
    xm_ref[0] = x
    h2_ref[0] = (_rms(x, n2_ref[...]) * (1.0 + sc2_ref[0]) + sh2_ref[0]).astype(BF16)


def _post(o, y_conv, zb, x, mod, row_mod, n2_g, w_ao, w_out, tm):
    nb, t, d = x.shape
    tile = lambda k: pl.BlockSpec((1, tm, d), lambda b, i: (b, i, k))
    return pl.pallas_call(
        _post_kernel,
        grid=(nb, t // tm),
        in_specs=[tile(0), tile(0), tile(C_GTA // d), tile(C_GTB // d), tile(0),
                  _mod_spec(row_mod, tm, 2), _mod_spec(row_mod, tm, 3),
                  _mod_spec(row_mod, tm, 4),
                  pl.BlockSpec((1, d), lambda b, i: (0, 0)),
                  _resident((d, d), lambda b, i: (0, 0)),
                  _resident((d, d), lambda b, i: (0, 0))],
        out_specs=(tile(0), tile(0)),
        out_shape=(jax.ShapeDtypeStruct((nb, t, d), F32), jax.ShapeDtypeStruct((nb, t, d), BF16)),
        compiler_params=_cparams(("arbitrary", "arbitrary")),
        name="merge_out_proj",
    )(o, y_conv, zb, zb, x, mod, mod, mod, n2_g.reshape(1, d), w_ao, w_out)


def _router_kernel(h_ref, wrt_ref, rb_ref, dest_ref, wts_ref, meta_ref, *, tm):
    lt = lax.dot_general(wrt_ref[...], h_ref[...], (((1,), (1,)), ((), ())),
                         preferred_element_type=F32) + rb_ref[...][:, 0:1]
    row = lambda r: lt[r:r + 1, :]
    best, gi = row(0), jnp.zeros((1, tm), I32)
    for k in range(1, N_GROUPS):
        upd = row(k) > best
        best = jnp.where(upd, row(k), best)
        gi = jnp.where(upd, k, gi)
    sg = jnp.zeros((1, tm), F32)
    for k in range(N_GROUPS):
        sg = sg + jnp.exp(row(k) - best)
    p_group = 1.0 / sg

    le = []
    for j in range(EXPERTS_PER_GROUP):
        v = row(N_GROUPS + j)
        for g in range(1, N_GROUPS):
            v = jnp.where(gi == g, row(N_GROUPS + g * EXPERTS_PER_GROUP + j), v)
        le.append(v)
    b1, i1 = le[0], jnp.zeros((1, tm), I32)
    for j in range(1, EXPERTS_PER_GROUP):
        upd = le[j] > b1
        b1 = jnp.where(upd, le[j], b1)
        i1 = jnp.where(upd, j, i1)
    b2, i2 = jnp.full((1, tm), -jnp.inf, F32), jnp.zeros((1, tm), I32)
    for j in range(EXPERTS_PER_GROUP):
        upd = jnp.where(i1 == j, -jnp.inf, le[j]) > b2
        b2 = jnp.where(upd, le[j], b2)
        i2 = jnp.where(upd, j, i2)
    p2 = jnp.exp(b2 - b1)
    w0 = p_group / (1.0 + p2)
    w1 = p_group * p2 / (1.0 + p2)
    e0 = gi * EXPERTS_PER_GROUP + i1
    e1 = gi * EXPERTS_PER_GROUP + i2

    eio = lax.broadcasted_iota(I32, (N_EXPERTS, tm), 0)
    hit0 = jnp.where(eio == e0, 1.0, 0.0)
    hit1 = jnp.where(eio == e1, 1.0, 0.0)
    et = hit0 + hit1
    counts = jnp.sum(et, axis=1, keepdims=True)
    units = jnp.floor((counts + (BF16_ROWS - 1)) * (1.0 / BF16_ROWS))
    lower = jnp.where(lax.broadcasted_iota(I32, (N_EXPERTS, N_EXPERTS), 1)
                      < lax.broadcasted_iota(I32, (N_EXPERTS, N_EXPERTS), 0), 1.0, 0.0)
    starts = jnp.dot(lower.astype(BF16), jnp.broadcast_to(units, (N_EXPERTS, LANES)).astype(BF16),
                     preferred_element_type=F32) * float(BF16_ROWS)
    before = jnp.where(lax.broadcasted_iota(I32, (tm, tm), 0)
                       < lax.broadcasted_iota(I32, (tm, tm), 1), 1.0, 0.0).astype(BF16)
    ranks = jnp.dot(et.astype(BF16), before, preferred_element_type=F32)
    slot = starts[:, 0:1] + ranks
    d0 = jnp.sum(hit0 * slot, axis=0, keepdims=True)
    d1 = jnp.sum(hit1 * slot, axis=0, keepdims=True)
    dest_ref[0] = jnp.zeros((8, tm), I32)
    dest_ref[0, 0:1, :] = d0.astype(I32)
    dest_ref[0, 1:2, :] = d1.astype(I32)
    wts_ref[0] = jnp.zeros((8, tm), F32)
    wts_ref[0, 0:1, :] = w0
    wts_ref[0, 1:2, :] = w1
    meta_ref[0, 0:N_EXPERTS, :] = starts.astype(I32)
    meta_ref[0, N_EXPERTS:2 * N_EXPERTS, :] = jnp.broadcast_to(counts, (N_EXPERTS, LANES)).astype(I32)


def _router(h2, wrt, rbias, tm):
    n, d = h2.shape
    nt = n // tm
    return pl.pallas_call(
        functools.partial(_router_kernel, tm=tm),
        grid=(nt,),
        in_specs=[pl.BlockSpec((tm, d), lambda i: (i, 0)),
                  pl.BlockSpec((LANES, d), lambda i: (0, 0)),
                  pl.BlockSpec((LANES, LANES), lambda i: (0, 0))],
        out_specs=(pl.BlockSpec((1, 8, tm), lambda i: (i, 0, 0)),
                   pl.BlockSpec((1, 8, tm), lambda i: (i, 0, 0)),
                   pl.BlockSpec((1, 2 * N_EXPERTS, LANES), lambda i: (i, 0, 0))),
        out_shape=(jax.ShapeDtypeStruct((nt, 8, tm), I32),
                   jax.ShapeDtypeStruct((nt, 8, tm), F32),
                   jax.ShapeDtypeStruct((nt, 2 * N_EXPERTS, LANES), I32)),
        compiler_params=_cparams(("arbitrary",)),
        name="moe_router",
    )(h2, wrt, rbias)


def _experts_kernel(meta_ref, h_ref, dest_ref, wts_ref, wg_ref, wu_ref, wd_ref, o_ref,
                    dall_ref, xs_ref, ws_ref, *, tm, nslot, rb):
    t = pl.program_id(0)
    e = pl.program_id(1)
    sb = 256

    @pl.when(e == 0)
    def _():
        d0 = dest_ref[0, 0:1, :]
        d1 = dest_ref[0, 1:2, :]
        w0 = wts_ref[0, 0:1, :]
        w1 = wts_ref[0, 1:2, :]
        for s0 in range(0, nslot, sb):
            sio = s0 + lax.broadcasted_iota(I32, (sb, 1), 0)
            m0 = sio == d0
            m1 = sio == d1
            dd = (jnp.where(m0, 1.0, 0.0) + jnp.where(m1, 1.0, 0.0)).astype(BF16)
            dall_ref[s0:s0 + sb, :] = dd
            wcol = jnp.sum(jnp.where(m0, w0, 0.0) + jnp.where(m1, w1, 0.0), axis=1, keepdims=True)
            ws_ref[s0:s0 + sb, :] = jnp.broadcast_to(wcol, (sb, LANES))
            xs_ref[s0:s0 + sb, :] = jnp.dot(dd, h_ref[...], preferred_element_type=F32).astype(BF16)
        xs_ref[nslot:nslot + rb, :] = jnp.zeros((rb, D_MODEL), BF16)
        ws_ref[nslot:nslot + rb, :] = jnp.zeros((rb, LANES), F32)

    start = meta_ref[t, e]
    n = meta_ref[t, N_EXPERTS + e]

    def run_block(j, carry):
        r0 = pl.multiple_of(start + j * rb, BF16_ROWS)
        xe = xs_ref[pl.ds(r0, rb), :]
        g = jnp.dot(xe, wg_ref[0], preferred_element_type=F32)
        u = jnp.dot(xe, wu_ref[0], preferred_element_type=F32)
        y = jnp.dot((_silu(g) * u).astype(BF16), wd_ref[0], preferred_element_type=F32)
        y = y * jnp.concatenate([ws_ref[pl.ds(r0, rb), :]] * (D_MODEL // LANES), axis=1)
        valid = lax.broadcasted_iota(I32, (rb, 1), 0) < n - j * rb
        xs_ref[pl.ds(r0, rb), :] = jnp.where(valid, y.astype(BF16), xe)
        return carry

    lax.fori_loop(0, (n + rb - 1) // rb, run_block, 0)

    @pl.when(e == N_EXPERTS - 1)
    def _():
        cw = 512
        for c0 in range(0, D_MODEL, cw):
            o_ref[:, c0:c0 + cw] = lax.dot_general(
                dall_ref[...], xs_ref[0:nslot, c0:c0 + cw], (((0,), (0,)), ((), ())),
                preferred_element_type=F32)


def _experts(h2, dest, wts, meta, w_g, w_u, w_d, tm):
    n, d = h2.shape
    nt = n // tm
    rb = 128
    nslot = -(-(2 * tm + N_EXPERTS * (BF16_ROWS - 1)) // 256) * 256
    grid_spec = pltpu.PrefetchScalarGridSpec(
        num_scalar_prefetch=1,
        grid=(nt, N_EXPERTS),
        in_specs=[_resident((tm, d), lambda t, e, m: (t, 0)),
                  pl.BlockSpec((1, 8, tm), lambda t, e, m: (t, 0, 0)),
                  pl.BlockSpec((1, 8, tm), lambda t, e, m: (t, 0, 0)),
                  pl.BlockSpec((1, d, D_EXPERT), lambda t, e, m: (e, 0, 0)),
                  pl.BlockSpec((1, d, D_EXPERT), lambda t, e, m: (e, 0, 0)),
                  pl.BlockSpec((1, D_EXPERT, d), lambda t, e, m: (e, 0, 0))],
        out_specs=pl.BlockSpec((tm, d), lambda t, e, m: (t, 0)),
        scratch_shapes=[pltpu.VMEM((nslot, tm), BF16),
                        pltpu.VMEM((nslot + rb, d), BF16),
                        pltpu.VMEM((nslot + rb, LANES), F32)])
    return pl.pallas_call(
        functools.partial(_experts_kernel, tm=tm, nslot=nslot, rb=rb),
        grid_spec=grid_spec,
        out_shape=jax.ShapeDtypeStruct((n, d), F32),
        compiler_params=_cparams(("arbitrary", "arbitrary")),
        name="moe_experts",
    )(meta, h2, dest, wts, w_g, w_u, w_d)


def _moe(h2, wrt, rbias, w_g, w_u, w_d, tm):
    dest, wts, meta = _router(h2, wrt, rbias, tm)
    return _experts(h2, dest, wts, meta[:, :, 0], w_g, w_u, w_d, tm)


def _final_kernel(x_ref, r_ref, gate_ref, g_ref, y_ref):
    y_ref[0] = _rms(x_ref[0] + gate_ref[0] * r_ref[0], g_ref[...])


def _final(x, resid, mod, row_mod, g, tm):
    nb, t, d = x.shape
    tile = pl.BlockSpec((1, tm, d), lambda b, i: (b, i, 0))
    return pl.pallas_call(
        _final_kernel,
        grid=(nb, t // tm),
        in_specs=[tile, tile, _mod_spec(row_mod, tm, 5), pl.BlockSpec((1, d), lambda b, i: (0, 0))],
        out_specs=tile,
        out_shape=jax.ShapeDtypeStruct((nb, t, d), F32),
        compiler_params=_cparams(("arbitrary", "arbitrary")),
        name="final_norm",
    )(x, resid, mod, g.reshape(1, d))


def _split_w_in(w):
    d = w.shape[0]
    parts_b = [w[:, 0:6144], w[:, 8272:12368], w[:, 7168:8192]]
    parts_kv = [w[:, 6144:7168], w[:, 8192:8272]]
    pad = lambda parts, n: parts + [jnp.zeros((d, n - sum(p.shape[1] for p in parts)), w.dtype)]
    return (jnp.concatenate(pad(parts_b, N_COLS_B), axis=1).astype(BF16),
            jnp.concatenate(pad(parts_kv, N_COLS_KV), axis=1).astype(BF16))


def _pick_tile(t, pref):
    tm = min(t, pref)
    assert t % tm == 0
    return tm


def kernel(x_prompt, x_sample, c_prompt, c_sample, cache_k, cache_v, cache_kidx, state_conv, norm1_g, norm2_g, w_mod, b_mod, w_in, conv_dw_w, conv_dw_b, conv_ln_g, conv_ln_b, w_conv_out, w_attn_out, w_out, router_g, router_g_b, router_e, router_e_b, w_gate, w_up, w_down, final_g):
    depth = w_in.shape[0]
    bp, seq, d = x_prompt.shape
    bs, dseq, _ = x_sample.shape
    past = cache_k.shape[2]
    n_keep = min(seq, past)
    dkv = N_KV_HEADS * HEAD_DIM
    kb = 512

    nc = bp + bs
    c_all = jnp.concatenate([c_prompt, c_sample, jnp.zeros((-nc % 8, d), F32)], axis=0)
    mod_all = _modulation(c_all, w_mod, b_mod)

    xp = x_prompt
    xs = x_sample.reshape(1, bs * dseq, d)
    res_p = res_s = None
    prev_mp = prev_ms = None
    outs = [[] for _ in range(8)]
    zero_buf = jnp.zeros((bp, HALO, d), F32)
    s_all = past + dseq
    s_pad = -(-s_all // (2 * kb)) * (2 * kb)

    for l in range(depth):
        mod_p = mod_all[l, :bp][:, None, :]
        mod_s = jnp.repeat(mod_all[l, bp:nc], dseq, axis=0)[None]
        wz_b, wz_kv = _split_w_in(w_in[l])
        w_pw = w_conv_out[l].astype(BF16)
        w_ao = w_attn_out[l].astype(BF16)
        w_o = w_out[l].astype(BF16)
        w_dw = jnp.broadcast_to(conv_dw_w[l][:, None, :], (CONV_W, SUBLANES, d))
        wrt = jnp.concatenate([router_g[l].T, router_e[l].T,
                               jnp.zeros((LANES - N_GROUPS - N_EXPERTS, d), F32)], axis=0).astype(BF16)
        rbias = jnp.broadcast_to(jnp.concatenate(
            [router_g_b[l], router_e_b[l], jnp.zeros((LANES - N_GROUPS - N_EXPERTS,), F32)])[:, None],
            (LANES, LANES))
        wg, wu, wd = w_gate[l].astype(BF16), w_up[l].astype(BF16), w_down[l].astype(BF16)

        hp, xp = _prenorm(xp, res_p, mod_p, prev_mp, False, norm1_g[l], _pick_tile(seq, 512))
        zp = _proj(hp, wz_b, BF16, _pick_tile(seq, 1024))
        zkv_p = _proj(hp, wz_kv, F32, _pick_tile(seq, 1024))
        yc_p, tail_p = _conv_branch(zp, zero_buf, w_dw, conv_dw_b[l], conv_ln_g[l], conv_ln_b[l],
                                    w_pw, _pick_tile(seq, 256))
        k_p = zkv_p[:, :, C_K:C_K + dkv]
        v_p = zkv_p[:, :, C_V:C_V + dkv]
        ki_p = zkv_p[:, :, C_KIWI:C_KIWI + IDX_DIM]
        o_p = _attention(zp, zkv_p, k_p.astype(BF16), v_p.astype(BF16), ki_p.astype(BF16),
                         _pick_tile(seq, 128), kb, True, seq)
        xm_p, h2_p = _post(o_p, yc_p, zp, xp, mod_p, False, norm2_g[l], w_ao, w_o, _pick_tile(seq, 256))
        res_p = _moe(h2_p.reshape(bp * seq, d), wrt, rbias, wg, wu, wd,
                     _pick_tile(bp * seq, 1024)).reshape(bp, seq, d)
        xp, prev_mp = xm_p, mod_p

        hs, xs = _prenorm(xs, res_s, mod_s, prev_ms, True, norm1_g[l], bs * dseq)
        zs = _proj(hs, wz_b, BF16, bs * dseq)
        zs_b = zs.reshape(bs, dseq, N_COLS_B)
        zkv_s = _proj(hs, wz_kv, F32, bs * dseq).reshape(bs, dseq, N_COLS_KV)
        init_s = jnp.concatenate([jnp.zeros((bs, HALO - (CONV_W - 1), d), F32), state_conv[l]], axis=1)
        yc_s, tail_s = _conv_branch(zs_b, init_s, w_dw, conv_dw_b[l], conv_ln_g[l], conv_ln_b[l],
                                    w_pw, dseq)
        k_s = zkv_s[:, :, C_K:C_K + dkv]
        v_s = zkv_s[:, :, C_V:C_V + dkv]
        ki_s = zkv_s[:, :, C_KIWI:C_KIWI + IDX_DIM]
        padk = jnp.zeros((bs, s_pad - s_all, dkv), BF16)
        k_all = jnp.concatenate([cache_k[l].reshape(bs, past, dkv).astype(BF16), k_s.astype(BF16), padk], axis=1)
        v_all = jnp.concatenate([cache_v[l].reshape(bs, past, dkv).astype(BF16), v_s.astype(BF16), padk], axis=1)
        ki_all = jnp.concatenate([cache_kidx[l].astype(BF16), ki_s.astype(BF16),
                                  jnp.zeros((bs, s_pad - s_all, IDX_DIM), BF16)], axis=1)
        o_s = _attention(zs_b, zkv_s, k_all, v_all, ki_all, dseq, kb, False, s_all)
        xm_s, h2_s = _post(o_s.reshape(1, bs * dseq, d), yc_s.reshape(1, bs * dseq, d), zs, xs,
                           mod_s, True, norm2_g[l], w_ao, w_o, bs * dseq)
        res_s = _moe(h2_s.reshape(bs * dseq, d), wrt, rbias, wg, wu, wd,
                     bs * dseq).reshape(1, bs * dseq, d)
        xs, prev_ms = xm_s, mod_s

        outs[0].append(k_p[:, seq - n_keep:].reshape(bp, n_keep, N_KV_HEADS, HEAD_DIM))
        outs[1].append(v_p[:, seq - n_keep:].reshape(bp, n_keep, N_KV_HEADS, HEAD_DIM))
        outs[2].append(ki_p[:, seq - n_keep:])
        outs[3].append(tail_p[:, HALO - (CONV_W - 1):])
        outs[4].append(k_s.reshape(bs, dseq, N_KV_HEADS, HEAD_DIM))
        outs[5].append(v_s.reshape(bs, dseq, N_KV_HEADS, HEAD_DIM))
        outs[6].append(ki_s)
        outs[7].append(tail_s[:, HALO - (CONV_W - 1):])

    y_p = _final(xp, res_p, prev_mp, False, final_g, _pick_tile(seq, 512))
    y_s = _final(xs, res_s, prev_ms, True, final_g, bs * dseq).reshape(bs, dseq, d)
    return (y_p, y_s) + tuple(jnp.stack(o) for o in outs)
```

```python
import functools
import math

import jax
import jax.numpy as jnp
from jax import lax
from jax.experimental import pallas as pl
from jax.experimental.pallas import tpu as pltpu

F32 = jnp.float32
BF16 = jnp.bfloat16
I32 = jnp.int32

D_MODEL = 2048
CHUNK = 64
CHUNK_SHIFT = 6
CONV_W = 31
N_HEADS = 16
N_KV_HEADS = 4
HEAD_DIM = 128
N_IDX_HEADS = 16
IDX_DIM = 64
TOPK_MAX = 256
N_GROUPS = 4
EXPERTS_PER_GROUP = 8
N_EXPERTS = N_GROUPS * EXPERTS_PER_GROUP
D_EXPERT = 512
EPS = 1e-6

LANES = 128
SUBLANES = 8
BF16_ROWS = 16
VMEM_LIMIT = 60 * 1024 * 1024

C_GA, C_GB, C_Q, C_GTA, C_GTB, C_QI = 0, 2048, 4096, 6144, 8192, 10240
N_COLS_B = 11520
C_K, C_V, C_KIWI = 0, 512, 1024
N_COLS_KV = 1280
PROJ_TN = 1280
HALO = 32

LOG2_E = 1.4426950408889634
NEG_BIG = -1e30
INT_MIN = -(2 ** 31)
NEG_INF_KEY = INT_MIN + 0x7FFFFF


def _cparams(sem):
    return pltpu.CompilerParams(dimension_semantics=sem, vmem_limit_bytes=VMEM_LIMIT)


def _sigmoid(x):
    return 1.0 / (1.0 + jnp.exp(-x))


def _silu(x):
    return x * _sigmoid(x)


def _f2k(x):
    b = pltpu.bitcast(x, I32)
    return b ^ ((b >> 31) & 0x7FFFFFFF)


def _k2f(k):
    return pltpu.bitcast(k ^ ((k >> 31) & 0x7FFFFFFF), F32)


def _resident(shape, index_map):
    return pl.BlockSpec(shape, index_map, pipeline_mode=pl.Buffered(1))


def _mod_kernel(c_ref, w_ref, b_ref, o_ref):
    s = _silu(c_ref[...]).astype(BF16)
    o_ref[0] = jnp.dot(s, w_ref[0].astype(BF16), preferred_element_type=F32) + b_ref[0]


def _modulation(c_all, w_mod, b_mod):
    depth, d, n6 = w_mod.shape
    rows = c_all.shape[0]
    tn = 1024
    return pl.pallas_call(
        _mod_kernel,
        grid=(depth, n6 // tn),
        in_specs=[pl.BlockSpec((rows, d), lambda l, j: (0, 0)),
                  pl.BlockSpec((1, d, tn), lambda l, j: (l, 0, j)),
                  pl.BlockSpec((1, 1, tn), lambda l, j: (l, 0, j))],
        out_specs=pl.BlockSpec((1, rows, tn), lambda l, j: (l, 0, j)),
        out_shape=jax.ShapeDtypeStruct((depth, rows, n6), F32),
        compiler_params=_cparams(("arbitrary", "arbitrary")),
        name="modulation",
    )(c_all, w_mod, b_mod.reshape(depth, 1, n6))


def _mod_spec(row_mod, tm, k):
    if row_mod:
        return pl.BlockSpec((1, tm, D_MODEL), lambda b, i: (b, i, k))
    return pl.BlockSpec((1, 1, D_MODEL), lambda b, i: (b, 0, k))


def _rms(x, g):
    ms = jnp.mean(x * x, axis=-1, keepdims=True)
    return x * lax.rsqrt(ms + EPS) * g


def _prenorm_kernel(*refs, has_resid):
    if has_resid:
        x_ref, r_ref, gate_ref, sh_ref, sc_ref, g_ref, xo_ref, h_ref = refs
        x = x_ref[0] + gate_ref[0] * r_ref[0]
        xo_ref[0] = x
    else:
        x_ref, sh_ref, sc_ref, g_ref, h_ref = refs
        x = x_ref[0]
    h_ref[0] = (_rms(x, g_ref[...]) * (1.0 + sc_ref[0]) + sh_ref[0]).astype(BF16)


def _prenorm(x, resid, mod, prev_mod, row_mod, g, tm):
    nb, t, d = x.shape
    has_resid = resid is not None
    xspec = pl.BlockSpec((1, tm, d), lambda b, i: (b, i, 0))
    in_specs, args = [xspec], [x]
    if has_resid:
        in_specs += [xspec, _mod_spec(row_mod, tm, 5)]
        args += [resid, prev_mod]
    in_specs += [_mod_spec(row_mod, tm, 0), _mod_spec(row_mod, tm, 1),
                 pl.BlockSpec((1, d), lambda b, i: (0, 0))]
    args += [mod, mod, g.reshape(1, d)]
    hshape = jax.ShapeDtypeStruct((nb, t, d), BF16)
    if has_resid:
        out_specs, out_shape = (xspec, xspec), (jax.ShapeDtypeStruct((nb, t, d), F32), hshape)
    else:
        out_specs, out_shape = xspec, hshape
    out = pl.pallas_call(
        functools.partial(_prenorm_kernel, has_resid=has_resid),
        grid=(nb, t // tm),
        in_specs=in_specs, out_specs=out_specs, out_shape=out_shape,
        compiler_params=_cparams(("arbitrary", "arbitrary")),
        name="prenorm",
    )(*args)
    return (out[1], out[0]) if has_resid else (out, x)


def _proj_kernel(h_ref, w_ref, o_ref):
    o_ref[0] = jnp.dot(h_ref[0], w_ref[...], preferred_element_type=F32).astype(o_ref.dtype)


def _proj(h, w, out_dtype, tm):
    nb, t, d = h.shape
    n = w.shape[1]
    tn = PROJ_TN
    return pl.pallas_call(
        _proj_kernel,
        grid=(n // tn, nb, t // tm),
        in_specs=[pl.BlockSpec((1, tm, d), lambda j, b, i: (b, i, 0)),
                  pl.BlockSpec((d, tn), lambda j, b, i: (0, j))],
        out_specs=pl.BlockSpec((1, tm, tn), lambda j, b, i: (b, i, j)),
        out_shape=jax.ShapeDtypeStruct((nb, t, n), out_dtype),
        compiler_params=_cparams(("arbitrary", "arbitrary", "arbitrary")),
        name="inproj",
    )(h, w)


def _conv_kernel(a_ref, b_ref, ha_ref, hb_ref, init_ref, wdw_ref, bdw_ref, lng_ref, lnb_ref,
                 wpw_ref, y_ref, tail_ref, full_ref, cv_ref, *, tm):
    i = pl.program_id(1)
    full_ref[HALO:HALO + tm, :] = a_ref[0].astype(F32) * _sigmoid(b_ref[0].astype(F32))

    @pl.when(i == 0)
    def _():
        full_ref[0:HALO, :] = init_ref[0]

    @pl.when(i > 0)
    def _():
        full_ref[0:HALO, :] = ha_ref[0].astype(F32) * _sigmoid(hb_ref[0].astype(F32))

    tail_ref[0] = full_ref[tm:tm + HALO, :]

    cc = 256
    rb = min(tm, 64)
    off = HALO - (CONV_W - 1)
    for c0 in range(0, D_MODEL, cc):
        for r0 in range(0, tm, rb):
            acc = jnp.zeros((rb, cc), F32)
            for s in range(SUBLANES):
                taps = [j for j in range(CONV_W) if (off + j) % SUBLANES == s]
                span = max((off + j) // SUBLANES for j in taps) * SUBLANES + rb
                win = full_ref[r0 + s:r0 + s + span, c0:c0 + cc]
                for j in taps:
                    a0 = (off + j) // SUBLANES * SUBLANES
                    wj = jnp.concatenate([wdw_ref[j, :, c0:c0 + cc]] * (rb // SUBLANES), axis=0)
                    acc = acc + wj * win[a0:a0 + rb]
            cv_ref[r0:r0 + rb, c0:c0 + cc] = acc + bdw_ref[:, c0:c0 + cc]

    y = cv_ref[...]
    mu = jnp.mean(y, axis=-1, keepdims=True)
    yc = y - mu
    var = jnp.mean(yc * yc, axis=-1, keepdims=True)
    yn = yc * lax.rsqrt(var + EPS) * lng_ref[...] + lnb_ref[...]
    y_ref[0] = jnp.dot(_silu(yn).astype(BF16), wpw_ref[...], preferred_element_type=F32)


def _conv_branch(zb, init_buf, w_dw, b_dw, ln_g, ln_b, w_pw, tm):
    nb, t, _ = zb.shape
    d = D_MODEL
    hb = tm // HALO
    tile = lambda k: pl.BlockSpec((1, tm, d), lambda b, i: (b, i, k))
    halo = lambda k: pl.BlockSpec((1, HALO, d), lambda b, i: (b, jnp.maximum(i * hb - 1, 0), k))
    vec = pl.BlockSpec((1, d), lambda b, i: (0, 0))
    return pl.pallas_call(
        functools.partial(_conv_kernel, tm=tm),
        grid=(nb, t // tm),
        in_specs=[tile(C_GA // d), tile(C_GB // d), halo(C_GA // d), halo(C_GB // d),
                  pl.BlockSpec((1, HALO, d), lambda b, i: (b, 0, 0)),
                  pl.BlockSpec((CONV_W, SUBLANES, d), lambda b, i: (0, 0, 0)),
                  vec, vec, vec, _resident((d, d), lambda b, i: (0, 0))],
        out_specs=(pl.BlockSpec((1, tm, d), lambda b, i: (b, i, 0)),
                   pl.BlockSpec((1, HALO, d), lambda b, i: (b, 0, 0))),
        out_shape=(jax.ShapeDtypeStruct((nb, t, d), F32),
                   jax.ShapeDtypeStruct((nb, HALO, d), F32)),
        scratch_shapes=[pltpu.VMEM((tm + HALO, d), F32), pltpu.VMEM((tm, d), F32)],
        compiler_params=_cparams(("arbitrary", "arbitrary")),
        name="conv_branch",
    )(zb, zb, zb, zb, init_buf, w_dw, b_dw.reshape(1, d), ln_g.reshape(1, d), ln_b.reshape(1, d), w_pw)


def _attn_kernel(q_ref, qi_ref, kw_ref, k_hbm, v_hbm, ki_ref, o_ref,
                 keys_ref, qs_ref, qis_ref, wb_ref, m_ref, acc_ref, sc_ref, kbuf, vbuf, kv_sem,
                 *, tq, kb, s_pad, causal, s_valid, n_sel):
    bi = pl.program_id(0)
    i = pl.program_id(1)
    q0 = i * tq
    gq = N_HEADS // N_KV_HEADS
    nlb = kb // LANES
    kbs = 2 * kb
    if causal:
        nkb = (q0 + tq + kb - 1) // kb
        nkbs = (q0 + tq + kbs - 1) // kbs
    else:
        nkb = s_pad // kb
        nkbs = s_pad // kbs

    def k_copy(blk):
        c0 = pl.multiple_of(blk * kb, kb)
        slot = lax.rem(blk, 3)
        return pltpu.make_async_copy(k_hbm.at[bi, pl.ds(c0, kb), :], kbuf.at[slot], kv_sem.at[0, slot])

    def v_copy(blk):
        c0 = pl.multiple_of(blk * kb, kb)
        slot = lax.rem(blk, 2)
        return pltpu.make_async_copy(v_hbm.at[bi, pl.ds(c0, kb), :], vbuf.at[slot], kv_sem.at[1, slot])

    k_copy(0).start()
    v_copy(0).start()

    @pl.when(nkb > 1)
    def _():
        k_copy(1).start()

    for h in range(N_HEADS):
        g, j = divmod(h, gq)
        qs_ref[g, j * tq:(j + 1) * tq, :] = q_ref[0, :, h * HEAD_DIM:(h + 1) * HEAD_DIM].astype(BF16)
    kw = kw_ref[0]
    for h in range(N_IDX_HEADS):
        qis_ref[h * tq:(h + 1) * tq, :] = (
            qi_ref[0, :, h * IDX_DIM:(h + 1) * IDX_DIM] * (IDX_DIM ** -0.5)).astype(BF16)
        wcol = kw[:, IDX_DIM + h:IDX_DIM + h + 1] * (N_IDX_HEADS ** -0.5)
        wb_ref[h] = jnp.broadcast_to(wcol, (tq, LANES))

    rows = q0 + lax.broadcasted_iota(I32, (tq, 1), 0)
    if causal:
        limit = ((rows >> CHUNK_SHIFT) + 1) << CHUNK_SHIFT
    else:
        limit = jnp.full((tq, 1), s_valid, I32)

    sub = 256

    def score_block(b, carry):
        m1, m2 = carry
        for u in range(kbs // sub):
            c0 = pl.multiple_of(b * kbs + u * sub, sub)
            d = lax.dot_general(qis_ref[...], ki_ref[0, pl.ds(c0, sub), :], (((1,), (1,)), ((), ())),
                                preferred_element_type=F32)
            acc = jnp.zeros((tq, sub), F32)
            for h in range(N_IDX_HEADS):
                w = jnp.concatenate([wb_ref[h]] * (sub // LANES), axis=1)
                acc = acc + w * jnp.maximum(d[h * tq:(h + 1) * tq], 0.0)
            pos = c0 + lax.broadcasted_iota(I32, (1, sub), 1)
            acc = jnp.where(pos < limit, acc, -jnp.inf)
            for v in range(sub // LANES):
                x = acc[:, v * LANES:(v + 1) * LANES]
                m2 = jnp.maximum(m2, jnp.minimum(m1, x))
                m1 = jnp.maximum(m1, x)
            keys_ref[:, pl.ds(c0, sub)] = _f2k(acc)
        return m1, m2

    neg = jnp.full((tq, LANES), -jnp.inf, F32)
    m1, m2 = lax.fori_loop(0, nkbs, score_block, (neg, neg))

    def count_ge(thr):
        def body(b, c):
            blk = keys_ref[:, pl.ds(pl.multiple_of(b * kb, kb), kb)]
            hit = jnp.where(blk >= thr, 1.0, 0.0)
            for u in range(nlb):
                c = c + hit[:, u * LANES:(u + 1) * LANES]
            return c
        c = lax.fori_loop(0, nkb, body, jnp.zeros((tq, LANES), F32))
        return jnp.sum(c, axis=1, keepdims=True)

    nsf = float(n_sel)
    log_n = math.log(n_sel)
    one = jnp.ones((tq, 1), F32)

    def live(lo, hi, settled):
        return jnp.where(settled < 0.5, jnp.where(hi > lo + 1, 1.0, 0.0), 0.0)

    def search_cond(st):
        it, lo, hi, settled = st[0], st[1], st[2], st[8]
        return jnp.logical_and(it < 100, jnp.max(live(lo, hi, settled)) > 0.5)

    def search_step(st):
        it, lo, hi, clo, chi, wlo, whi, last, settled, thr = st
        flo = (jnp.log(clo + 0.5) - log_n) * wlo
        fhi = (jnp.log(chi + 0.5) - log_n) * whi
        ck = _f2k((_k2f(lo) * (-fhi) + _k2f(hi) * flo) / (flo - fhi))
        mid = (lo & hi) + ((lo ^ hi) >> 1)
        inside = jnp.where(ck > lo, jnp.where(ck < hi, 1.0, 0.0), 0.0)
        take_mid = jnp.where(lax.rem(it, 3) == 2, 0.0, inside) < 0.5
        cand = jnp.where(take_mid, mid, ck)
        cnt = count_ge(cand)
        act = live(lo, hi, settled)
        up_lo = jnp.where(cnt >= nsf, act, 0.0) > 0.5
        up_hi = jnp.where(cnt >= nsf, 0.0, act) > 0.5
        hit = jnp.where(cnt == nsf, act, 0.0) > 0.5
        thr = jnp.where(hit, cand, thr)
        settled = jnp.where(hit, 1.0, settled)
        whi_n = jnp.where(up_lo, jnp.where(last == 1.0, whi * 0.5, whi), jnp.where(up_hi, 1.0, whi))
        wlo_n = jnp.where(up_hi, jnp.where(last == 2.0, wlo * 0.5, wlo), jnp.where(up_lo, 1.0, wlo))
        last = jnp.where(up_lo, 1.0, jnp.where(up_hi, 2.0, last))
        return (it + 1, jnp.where(up_lo, cand, lo), jnp.where(up_hi, cand, hi),
                jnp.where(up_lo, cnt, clo), jnp.where(up_hi, cnt, chi), wlo_n, whi_n, last, settled, thr)

    st = lax.while_loop(search_cond, search_step, (
        jnp.int32(0),
        _f2k(jnp.min(m2, axis=1, keepdims=True)),
        _f2k(jnp.max(m1, axis=1, keepdims=True)) + 1,
        2.0 * nsf * one, 0.0 * one, one, one, 0.0 * one,
        jnp.where(limit <= n_sel, 1.0, 0.0),
        jnp.full((tq, 1), NEG_INF_KEY + 1, I32)))
    settled = st[8]
    thr = jnp.maximum(jnp.where(settled > 0.5, st[9], st[1]), NEG_INF_KEY + 1)

    @pl.when(jnp.min(settled) < 0.5)
    def _():
        excess = count_ge(thr) - float(n_sel)

        @pl.when(jnp.max(excess) > 0.0)
        def _():
            def count_eq(b, c):
                blk = keys_ref[:, pl.ds(pl.multiple_of(b * kb, kb), kb)]
                hit = jnp.where(blk == thr, 1.0, 0.0)
                for u in range(nlb):
                    c = c + hit[:, u * LANES:(u + 1) * LANES]
                return c
            n_eq = jnp.sum(lax.fori_loop(0, nkb, count_eq, jnp.zeros((tq, LANES), F32)),
                           axis=1, keepdims=True)
            keep = jnp.where(excess > 0.0, n_eq - excess, n_eq)
            tri = jnp.where(lax.broadcasted_iota(I32, (LANES, LANES), 0)
                            <= lax.broadcasted_iota(I32, (LANES, LANES), 1), 1.0, 0.0).astype(BF16)

            def drop_block(b, run):
                c0 = pl.multiple_of(b * LANES, LANES)
                blk = keys_ref[:, pl.ds(c0, LANES)]
                eq = blk == thr
                eqf = jnp.where(eq, 1.0, 0.0)
                rank = run + jnp.dot(eqf.astype(BF16), tri, preferred_element_type=F32) - 1.0
                keys_ref[:, pl.ds(c0, LANES)] = jnp.where(eq & (rank >= keep), INT_MIN, blk)
                return run + jnp.sum(eqf, axis=1, keepdims=True)

            lax.fori_loop(0, nkb * nlb, drop_block, jnp.zeros((tq, 1), F32))

    m_ref[...] = jnp.full(m_ref.shape, NEG_BIG, F32)
    acc_ref[...] = jnp.zeros(acc_ref.shape, F32)
    scale = (HEAD_DIM ** -0.5) * LOG2_E
    ones = jnp.ones((kb, HEAD_DIM), BF16)

    def qk(g, blk):
        kg = kbuf[lax.rem(blk, 3), :, g * HEAD_DIM:(g + 1) * HEAD_DIM]
        return lax.dot_general(qs_ref[g], kg, (((1,), (1,)), ((), ())), preferred_element_type=F32)

    k_copy(0).wait()
    for g in range(N_KV_HEADS):
        sc_ref[g] = qk(g, 0)

    def attend_block(b, carry):
        c0 = pl.multiple_of(b * kb, kb)
        nxt = jnp.minimum(b + 1, nkb - 1)

        @pl.when(b + 1 < nkb)
        def _():
            k_copy(b + 1).wait()

        v_copy(b).wait()

        @pl.when(b + 2 < nkb)
        def _():
            k_copy(b + 2).start()

        @pl.when(b + 1 < nkb)
        def _():
            v_copy(b + 1).start()

        bias = jnp.where(keys_ref[:, pl.ds(c0, kb)] >= thr, 0.0, NEG_BIG)
        bias = jnp.concatenate([bias] * gq, axis=0)
        vslot = lax.rem(b, 2)
        for g in range(N_KV_HEADS):
            s = sc_ref[g] * scale + bias
            sc_ref[g] = qk(g, nxt)
            vg = vbuf[vslot, :, g * HEAD_DIM:(g + 1) * HEAD_DIM]
            m_prev = m_ref[g]
            m_new = jnp.maximum(m_prev, jnp.max(s, axis=1, keepdims=True))
            alpha = jnp.exp2(m_prev - m_new)
            p = jnp.exp2(s - jnp.concatenate([m_new] * nlb, axis=1))
            pv = jnp.dot(p.astype(BF16), jnp.concatenate([vg, ones], axis=1),
                         preferred_element_type=F32)
            acc_ref[g] = jnp.concatenate([alpha, alpha], axis=1) * acc_ref[g] + pv
            m_ref[g] = m_new
        return carry

    lax.fori_loop(0, nkb, attend_block, 0)

    for h in range(N_HEADS):
        g, j = divmod(h, gq)
        o = acc_ref[g, j * tq:(j + 1) * tq, 0:HEAD_DIM] / acc_ref[g, j * tq:(j + 1) * tq, HEAD_DIM:2 * HEAD_DIM]
        o_ref[0, :, h * HEAD_DIM:(h + 1) * HEAD_DIM] = o.astype(BF16)


def _attention(zb, zkv, kb16, vb16, ki16, tq, kb, causal, s_valid):
    nb, t, _ = zb.shape
    s_pad = kb16.shape[1]
    assert s_pad % (2 * kb) == 0
    assert min(TOPK_MAX, s_valid // 4) <= 2 * LANES
    dq = N_HEADS * HEAD_DIM
    dqi = N_IDX_HEADS * IDX_DIM
    dkv = N_KV_HEADS * HEAD_DIM
    gq = N_HEADS // N_KV_HEADS
    return pl.pallas_call(
        functools.partial(_attn_kernel, tq=tq, kb=kb, s_pad=s_pad, causal=causal, s_valid=s_valid,
                          n_sel=min(TOPK_MAX, s_valid // 4)),
        grid=(nb, t // tq),
        in_specs=[pl.BlockSpec((1, tq, dq), lambda b, i: (b, i, C_Q // dq)),
                  pl.BlockSpec((1, tq, dqi), lambda b, i: (b, i, C_QI // dqi)),
                  pl.BlockSpec((1, tq, LANES), lambda b, i: (b, i, C_KIWI // LANES)),
                  pl.BlockSpec(memory_space=pl.ANY),
                  pl.BlockSpec(memory_space=pl.ANY),
                  _resident((1, s_pad, IDX_DIM), lambda b, i: (b, 0, 0))],
        out_specs=pl.BlockSpec((1, tq, dq), lambda b, i: (b, i, 0)),
        out_shape=jax.ShapeDtypeStruct((nb, t, dq), BF16),
        scratch_shapes=[pltpu.VMEM((tq, s_pad), I32),
                        pltpu.VMEM((N_KV_HEADS, gq * tq, HEAD_DIM), BF16),
                        pltpu.VMEM((N_IDX_HEADS * tq, IDX_DIM), BF16),
                        pltpu.VMEM((N_IDX_HEADS, tq, LANES), F32),
                        pltpu.VMEM((N_KV_HEADS, gq * tq, LANES), F32),
                        pltpu.VMEM((N_KV_HEADS, gq * tq, 2 * HEAD_DIM), F32),
                        pltpu.VMEM((N_KV_HEADS, gq * tq, kb), F32),
                        pltpu.VMEM((3, kb, dkv), BF16),
                        pltpu.VMEM((2, kb, dkv), BF16),
                        pltpu.SemaphoreType.DMA((2, 3))],
        compiler_params=_cparams(("arbitrary", "arbitrary")),
        name="sparse_attention",
    )(zb, zb, zkv, kb16, vb16, ki16)


def _post_kernel(o_ref, yc_ref, ga_ref, gb_ref, x_ref, g1_ref, sh2_ref, sc2_ref, n2_ref,
                 wao_ref, wout_ref, xm_ref, h2_ref):
    ya = jnp.dot(o_ref[0], wao_ref[...], preferred_element_type=F32)
    m = _sigmoid(ga_ref[0].astype(F32)) * yc_ref[0] + _sigmoid(gb_ref[0].astype(F32)) * ya
    mix = jnp.dot(m.astype(BF16), wout_ref[...], preferred_element_type=F32)
    x = x_ref[0] + g1_ref[0] * mix
    xm_ref[0] = x
    h2_ref[0] = (_rms(x, n2_ref[...]) * (1.0 + sc2_ref[0]) + sh2_ref[0]).astype(BF16)


def _post(o, y_conv, zb, x, mod, row_mod, n2_g, w_ao, w_out, tm):
    nb, t, d = x.shape
    tile = lambda k: pl.BlockSpec((1, tm, d), lambda b, i: (b, i, k))
    return pl.pallas_call(
        _post_kernel,
        grid=(nb, t // tm),
        in_specs=[tile(0), tile(0), tile(C_GTA // d), tile(C_GTB // d), tile(0),
                  _mod_spec(row_mod, tm, 2), _mod_spec(row_mod, tm, 3),
                  _mod_spec(row_mod, tm, 4),
                  pl.BlockSpec((1, d), lambda b, i: (0, 0)),
                  _resident((d, d), lambda b, i: (0, 0)),
                  _resident((d, d), lambda b, i: (0, 0))],
        out_specs=(tile(0), tile(0)),
        out_shape=(jax.ShapeDtypeStruct((nb, t, d), F32), jax.ShapeDtypeStruct((nb, t, d), BF16)),
        compiler_params=_cparams(("arbitrary", "arbitrary")),
        name="merge_out_proj",
    )(o, y_conv, zb, zb, x, mod, mod, mod, n2_g.reshape(1, d), w_ao, w_out)


def _router_kernel(h_ref, wrt_ref, rb_ref, dest_ref, wts_ref, meta_ref, *, tm):
    lt = lax.dot_general(wrt_ref[...], h_ref[...], (((1,), (1,)), ((), ())),
                         preferred_element_type=F32) + rb_ref[...][:, 0:1]
    row = lambda r: lt[r:r + 1, :]
    best, gi = row(0), jnp.zeros((1, tm), I32)
    for k in range(1, N_GROUPS):
        upd = row(k) > best
        best = jnp.where(upd, row(k), best)
        gi = jnp.where(upd, k, gi)
    sg = jnp.zeros((1, tm), F32)
    for k in range(N_GROUPS):
        sg = sg + jnp.exp(row(k) - best)
    p_group = 1.0 / sg

    le = []
    for j in range(EXPERTS_PER_GROUP):
        v = row(N_GROUPS + j)
        for g in range(1, N_GROUPS):
            v = jnp.where(gi == g, row(N_GROUPS + g * EXPERTS_PER_GROUP + j), v)
        le.append(v)
    b1, i1 = le[0], jnp.zeros((1, tm), I32)
    for j in range(1, EXPERTS_PER_GROUP):
        upd = le[j] > b1
        b1 = jnp.where(upd, le[j], b1)
        i1 = jnp.where(upd, j, i1)
    b2, i2 = jnp.full((1, tm), -jnp.inf, F32), jnp.zeros((1, tm), I32)
    for j in range(EXPERTS_PER_GROUP):
        upd = jnp.where(i1 == j, -jnp.inf, le[j]) > b2
        b2 = jnp.where(upd, le[j], b2)
        i2 = jnp.where(upd, j, i2)
    p2 = jnp.exp(b2 - b1)
    w0 = p_group / (1.0 + p2)
    w1 = p_group * p2 / (1.0 + p2)
    e0 = gi * EXPERTS_PER_GROUP + i1
    e1 = gi * EXPERTS_PER_GROUP + i2

    eio = lax.broadcasted_iota(I32, (N_EXPERTS, tm), 0)
    hit0 = jnp.where(eio == e0, 1.0, 0.0)
    hit1 = jnp.where(eio == e1, 1.0, 0.0)
    et = hit0 + hit1
    counts = jnp.sum(et, axis=1, keepdims=True)
    units = jnp.floor((counts + (BF16_ROWS - 1)) * (1.0 / BF16_ROWS))
    lower = jnp.where(lax.broadcasted_iota(I32, (N_EXPERTS, N_EXPERTS), 1)
                      < lax.broadcasted_iota(I32, (N_EXPERTS, N_EXPERTS), 0), 1.0, 0.0)
    starts = jnp.dot(lower.astype(BF16), jnp.broadcast_to(units, (N_EXPERTS, LANES)).astype(BF16),
                     preferred_element_type=F32) * float(BF16_ROWS)
    before = jnp.where(lax.broadcasted_iota(I32, (tm, tm), 0)
                       < lax.broadcasted_iota(I32, (tm, tm), 1), 1.0, 0.0).astype(BF16)
    ranks = jnp.dot(et.astype(BF16), before, preferred_element_type=F32)
    slot = starts[:, 0:1] + ranks
    d0 = jnp.sum(hit0 * slot, axis=0, keepdims=True)
    d1 = jnp.sum(hit1 * slot, axis=0, keepdims=True)
    dest_ref[0] = jnp.zeros((8, tm), I32)
    dest_ref[0, 0:1, :] = d0.astype(I32)
    dest_ref[0, 1:2, :] = d1.astype(I32)
    wts_ref[0] = jnp.zeros((8, tm), F32)
    wts_ref[0, 0:1, :] = w0
    wts_ref[0, 1:2, :] = w1
    meta_ref[0, 0:N_EXPERTS, :] = starts.astype(I32)
    meta_ref[0, N_EXPERTS:2 * N_EXPERTS, :] = jnp.broadcast_to(counts, (N_EXPERTS, LANES)).astype(I32)


def _router(h2, wrt, rbias, tm):
    n, d = h2.shape
    nt = n // tm
    return pl.pallas_call(
        functools.partial(_router_kernel, tm=tm),
        grid=(nt,),
        in_specs=[pl.BlockSpec((tm, d), lambda i: (i, 0)),
                  pl.BlockSpec((LANES, d), lambda i: (0, 0)),
                  pl.BlockSpec((LANES, LANES), lambda i: (0, 0))],
        out_specs=(pl.BlockSpec((1, 8, tm), lambda i: (i, 0, 0)),
                   pl.BlockSpec((1, 8, tm), lambda i: (i, 0, 0)),
                   pl.BlockSpec((1, 2 * N_EXPERTS, LANES), lambda i: (i, 0, 0))),
        out_shape=(jax.ShapeDtypeStruct((nt, 8, tm), I32),
                   jax.ShapeDtypeStruct((nt, 8, tm), F32),
                   jax.ShapeDtypeStruct((nt, 2 * N_EXPERTS, LANES), I32)),
        compiler_params=_cparams(("arbitrary",)),
        name="moe_router",
    )(h2, wrt, rbias)


def _experts_kernel(meta_ref, h_ref, dest_ref, wts_ref, wg_ref, wu_ref, wd_ref, o_ref,
                    dall_ref, xs_ref, ws_ref, *, tm, nslot, rb):
    t = pl.program_id(0)
    e = pl.program_id(1)
    sb = 256

    @pl.when(e == 0)
    def _():
        d0 = dest_ref[0, 0:1, :]
        d1 = dest_ref[0, 1:2, :]
        w0 = wts_ref[0, 0:1, :]
        w1 = wts_ref[0, 1:2, :]
        for s0 in range(0, nslot, sb):
            sio = s0 + lax.broadcasted_iota(I32, (sb, 1), 0)
            m0 = sio == d0
            m1 = sio == d1
            dd = (jnp.where(m0, 1.0, 0.0) + jnp.where(m1, 1.0, 0.0)).astype(BF16)
            dall_ref[s0:s0 + sb, :] = dd
            wcol = jnp.sum(jnp.where(m0, w0, 0.0) + jnp.where(m1, w1, 0.0), axis=1, keepdims=True)
            ws_ref[s0:s0 + sb, :] = jnp.broadcast_to(wcol, (sb, LANES))
            xs_ref[s0:s0 + sb, :] = jnp.dot(dd, h_ref[...], preferred_element_type=F32).astype(BF16)
        xs_ref[nslot:nslot + rb, :] = jnp.zeros((rb, D_MODEL), BF16)
        ws_ref[nslot:nslot + rb, :] = jnp.zeros((rb, LANES), F32)

    start = meta_ref[t, e]
    n = meta_ref[t, N_EXPERTS + e]

    def run_block(j, carry):
        r0 = pl.multiple_of(start + j * rb, BF16_ROWS)
        xe = xs_ref[pl.ds(r0, rb), :]
        g = jnp.dot(xe, wg_ref[0], preferred_element_type=F32)
        u = jnp.dot(xe, wu_ref[0], preferred_element_type=F32)
        y = jnp.dot((_silu(g) * u).astype(BF16), wd_ref[0], preferred_element_type=F32)
        y = y * jnp.concatenate([ws_ref[pl.ds(r0, rb), :]] * (D_MODEL // LANES), axis=1)
        valid = lax.broadcasted_iota(I32, (rb, 1), 0) < n - j * rb
        xs_ref[pl.ds(r0, rb), :] = jnp.where(valid, y.astype(BF16), xe)
        return carry

    lax.fori_loop(0, (n + rb - 1) // rb, run_block, 0)

    @pl.when(e == N_EXPERTS - 1)
    def _():
        cw = 512
        for c0 in range(0, D_MODEL, cw):
            o_ref[:, c0:c0 + cw] = lax.dot_general(
                dall_ref[...], xs_ref[0:nslot, c0:c0 + cw], (((0,), (0,)), ((), ())),
                preferred_element_type=F32)


def _experts(h2, dest, wts, meta, w_g, w_u, w_d, tm):
    n, d = h2.shape
    nt = n // tm
    rb = 128
    nslot = -(-(2 * tm + N_EXPERTS * (BF16_ROWS - 1)) // 256) * 256
    grid_spec = pltpu.PrefetchScalarGridSpec(
        num_scalar_prefetch=1,
        grid=(nt, N_EXPERTS),
        in_specs=[_resident((tm, d), lambda t, e, m: (t, 0)),
                  pl.BlockSpec((1, 8, tm), lambda t, e, m: (t, 0, 0)),
                  pl.BlockSpec((1, 8, tm), lambda t, e, m: (t, 0, 0)),
                  pl.BlockSpec((1, d, D_EXPERT), lambda t, e, m: (e, 0, 0)),
                  pl.BlockSpec((1, d, D_EXPERT), lambda t, e, m: (e, 0, 0)),
                  pl.BlockSpec((1, D_EXPERT, d), lambda t, e, m: (e, 0, 0))],
        out_specs=pl.BlockSpec((tm, d), lambda t, e, m: (t, 0)),
        scratch_shapes=[pltpu.VMEM((nslot, tm), BF16),
                        pltpu.VMEM((nslot + rb, d), BF16),
                        pltpu.VMEM((nslot + rb, LANES), F32)])
    return pl.pallas_call(
        functools.partial(_experts_kernel, tm=tm, nslot=nslot, rb=rb),
        grid_spec=grid_spec,
        out_shape=jax.ShapeDtypeStruct((n, d), F32),
        compiler_params=_cparams(("arbitrary", "arbitrary")),
        name="moe_experts",
    )(meta, h2, dest, wts, w_g, w_u, w_d)


def _moe(h2, wrt, rbias, w_g, w_u, w_d, tm):
    dest, wts, meta = _router(h2, wrt, rbias, tm)
    return _experts(h2, dest, wts, meta[:, :, 0], w_g, w_u, w_d, tm)


def _final_kernel(x_ref, r_ref, gate_ref, g_ref, y_ref):
    y_ref[0] = _rms(x_ref[0] + gate_ref[0] * r_ref[0], g_ref[...])


def _final(x, resid, mod, row_mod, g, tm):
    nb, t, d = x.shape
    tile = pl.BlockSpec((1, tm, d), lambda b, i: (b, i, 0))
    return pl.pallas_call(
        _final_kernel,
        grid=(nb, t // tm),
        in_specs=[tile, tile, _mod_spec(row_mod, tm, 5), pl.BlockSpec((1, d), lambda b, i: (0, 0))],
        out_specs=tile,
        out_shape=jax.ShapeDtypeStruct((nb, t, d), F32),
        compiler_params=_cparams(("arbitrary", "arbitrary")),
        name="final_norm",
    )(x, resid, mod, g.reshape(1, d))


def _split_w_in(w):
    d = w.shape[0]
    parts_b = [w[:, 0:6144], w[:, 8272:12368], w[:, 7168:8192]]
    parts_kv = [w[:, 6144:7168], w[:, 8192:8272]]
    pad = lambda parts, n: parts + [jnp.zeros((d, n - sum(p.shape[1] for p in parts)), w.dtype)]
    return (jnp.concatenate(pad(parts_b, N_COLS_B), axis=1).astype(BF16),
            jnp.concatenate(pad(parts_kv, N_COLS_KV), axis=1).astype(BF16))


def _pick_tile(t, pref):
    tm = min(t, pref)
    assert t % tm == 0
    return tm


def kernel(x_prompt, x_sample, c_prompt, c_sample, cache_k, cache_v, cache_kidx, state_conv, norm1_g, norm2_g, w_mod, b_mod, w_in, conv_dw_w, conv_dw_b, conv_ln_g, conv_ln_b, w_conv_out, w_attn_out, w_out, router_g, router_g_b, router_e, router_e_b, w_gate, w_up, w_down, final_g):
    depth = w_in.shape[0]
    bp, seq, d = x_prompt.shape
    bs, dseq, _ = x_sample.shape
    past = cache_k.shape[2]
    n_keep = min(seq, past)
    dkv = N_KV_HEADS * HEAD_DIM
    kb = 512

    nc = bp + bs
    c_all = jnp.concatenate([c_prompt, c_sample, jnp.zeros((-nc % 8, d), F32)], axis=0)
    mod_all = _modulation(c_all, w_mod, b_mod)

    xp = x_prompt
    xs = x_sample.reshape(1, bs * dseq, d)
    res_p = res_s = None
    prev_mp = prev_ms = None
    outs = [[] for _ in range(8)]
    zero_buf = jnp.zeros((bp, HALO, d), F32)
    s_all = past + dseq
    s_pad = -(-s_all // (2 * kb)) * (2 * kb)

    for l in range(depth):
        mod_p = mod_all[l, :bp][:, None, :]
        mod_s = jnp.repeat(mod_all[l, bp:nc], dseq, axis=0)[None]
        wz_b, wz_kv = _split_w_in(w_in[l])
        w_pw = w_conv_out[l].astype(BF16)
        w_ao = w_attn_out[l].astype(BF16)
        w_o = w_out[l].astype(BF16)
        w_dw = jnp.broadcast_to(conv_dw_w[l][:, None, :], (CONV_W, SUBLANES, d))
        wrt = jnp.concatenate([router_g[l].T, router_e[l].T,
                               jnp.zeros((LANES - N_GROUPS - N_EXPERTS, d), F32)], axis=0).astype(BF16)
        rbias = jnp.broadcast_to(jnp.concatenate(
            [router_g_b[l], router_e_b[l], jnp.zeros((LANES - N_GROUPS - N_EXPERTS,), F32)])[:, None],
            (LANES, LANES))
        wg, wu, wd = w_gate[l].astype(BF16), w_up[l].astype(BF16), w_down[l].astype(BF16)

        hp, xp = _prenorm(xp, res_p, mod_p, prev_mp, False, norm1_g[l], _pick_tile(seq, 512))
        zp = _proj(hp, wz_b, BF16, _pick_tile(seq, 1024))
        zkv_p = _proj(hp, wz_kv, F32, _pick_tile(seq, 1024))
        yc_p, tail_p = _conv_branch(zp, zero_buf, w_dw, conv_dw_b[l], conv_ln_g[l], conv_ln_b[l],
                                    w_pw, _pick_tile(seq, 256))
        k_p = zkv_p[:, :, C_K:C_K + dkv]
        v_p = zkv_p[:, :, C_V:C_V + dkv]
        ki_p = zkv_p[:, :, C_KIWI:C_KIWI + IDX_DIM]
        o_p = _attention(zp, zkv_p, k_p.astype(BF16), v_p.astype(BF16), ki_p.astype(BF16),
                         _pick_tile(seq, 128), kb, True, seq)
        xm_p, h2_p = _post(o_p, yc_p, zp, xp, mod_p, False, norm2_g[l], w_ao, w_o, _pick_tile(seq, 256))
        res_p = _moe(h2_p.reshape(bp * seq, d), wrt, rbias, wg, wu, wd,
                     _pick_tile(bp * seq, 1024)).reshape(bp, seq, d)
        xp, prev_mp = xm_p, mod_p

        hs, xs = _prenorm(xs, res_s, mod_s, prev_ms, True, norm1_g[l], bs * dseq)
        zs = _proj(hs, wz_b, BF16, bs * dseq)
        zs_b = zs.reshape(bs, dseq, N_COLS_B)
        zkv_s = _proj(hs, wz_kv, F32, bs * dseq).reshape(bs, dseq, N_COLS_KV)
        init_s = jnp.concatenate([jnp.zeros((bs, HALO - (CONV_W - 1), d), F32), state_conv[l]], axis=1)
        yc_s, tail_s = _conv_branch(zs_b, init_s, w_dw, conv_dw_b[l], conv_ln_g[l], conv_ln_b[l],
                                    w_pw, dseq)
        k_s = zkv_s[:, :, C_K:C_K + dkv]
        v_s = zkv_s[:, :, C_V:C_V + dkv]
        ki_s = zkv_s[:, :, C_KIWI:C_KIWI + IDX_DIM]
        padk = jnp.zeros((bs, s_pad - s_all, dkv), BF16)
        k_all = jnp.concatenate([cache_k[l].reshape(bs, past, dkv).astype(BF16), k_s.astype(BF16), padk], axis=1)
        v_all = jnp.concatenate([cache_v[l].reshape(bs, past, dkv).astype(BF16), v_s.astype(BF16), padk], axis=1)
        ki_all = jnp.concatenate([cache_kidx[l].astype(BF16), ki_s.astype(BF16),
                                  jnp.zeros((bs, s_pad - s_all, IDX_DIM), BF16)], axis=1)
        o_s = _attention(zs_b, zkv_s, k_all, v_all, ki_all, dseq, kb, False, s_all)
        xm_s, h2_s = _post(o_s.reshape(1, bs * dseq, d), yc_s.reshape(1, bs * dseq, d), zs, xs,
                           mod_s, True, norm2_g[l], w_ao, w_o, bs * dseq)
        res_s = _moe(h2_s.reshape(bs * dseq, d), wrt, rbias, wg, wu, wd,
                     bs * dseq).reshape(1, bs * dseq, d)
        xs, prev_ms = xm_s, mod_s

        outs[0].append(k_p[:, seq - n_keep:].reshape(bp, n_keep, N_KV_HEADS, HEAD_DIM))
        outs[1].append(v_p[:, seq - n_keep:].reshape(bp, n_keep, N_KV_HEADS, HEAD_DIM))
        outs[2].append(ki_p[:, seq - n_keep:])
        outs[3].append(tail_p[:, HALO - (CONV_W - 1):])
        outs[4].append(k_s.reshape(bs, dseq, N_KV_HEADS, HEAD_DIM))
        outs[5].append(v_s.reshape(bs, dseq, N_KV_HEADS, HEAD_DIM))
        outs[6].append(ki_s)
        outs[7].append(tail_s[:, HALO - (CONV_W - 1):])

    y_p = _final(xp, res_p, prev_mp, False, final_g, _pick_tile(seq, 512))
    y_s = _final(xs, res_s, prev_ms, True, final_g, bs * dseq).reshape(bs, dseq, d)
    return (y_p, y_s) + tuple(jnp.stack(o) for o in outs)
```

```python
import functools
import math

import jax
import jax.numpy as jnp
from jax import lax
from jax.experimental import pallas as pl
from jax.experimental.pallas import tpu as pltpu

F32 = jnp.float32
BF16 = jnp.bfloat16
I32 = jnp.int32

D_MODEL = 2048
CHUNK = 64
CHUNK_SHIFT = 6
CONV_W = 31
N_HEADS = 16
N_KV_HEADS = 4
HEAD_DIM = 128
N_IDX_HEADS = 16
IDX_DIM = 64
TOPK_MAX = 256
N_GROUPS = 4
EXPERTS_PER_GROUP = 8
N_EXPERTS = N_GROUPS * EXPERTS_PER_GROUP
D_EXPERT = 512
EPS = 1e-6

LANES = 128
SUBLANES = 8
BF16_ROWS = 16
VMEM_LIMIT = 60 * 1024 * 1024

C_GA, C_GB, C_Q, C_GTA, C_GTB, C_QI = 0, 2048, 4096, 6144, 8192, 10240
N_COLS_B = 11520
C_K, C_V, C_KIWI = 0, 512, 1024
N_COLS_KV = 1280
PROJ_TN = 1280
HALO = 32

LOG2_E = 1.4426950408889634
NEG_BIG = -1e30
INT_MIN = -(2 ** 31)
NEG_INF_KEY = INT_MIN + 0x7FFFFF


def _cparams(sem):
    return pltpu.CompilerParams(dimension_semantics=sem, vmem_limit_bytes=VMEM_LIMIT)


def _sigmoid(x):
    return 1.0 / (1.0 + jnp.exp(-x))


def _silu(x):
    return x * _sigmoid(x)


def _f2k(x):
    b = pltpu.bitcast(x, I32)
    return b ^ ((b >> 31) & 0x7FFFFFFF)


def _k2f(k):
    return pltpu.bitcast(k ^ ((k >> 31) & 0x7FFFFFFF), F32)


def _resident(shape, index_map):
    return pl.BlockSpec(shape, index_map, pipeline_mode=pl.Buffered(1))


def _mod_kernel(c_ref, w_ref, b_ref, o_ref):
    s = _silu(c_ref[...]).astype(BF16)
    o_ref[0] = jnp.dot(s, w_ref[0].astype(BF16), preferred_element_type=F32) + b_ref[0]


def _modulation(c_all, w_mod, b_mod):
    depth, d, n6 = w_mod.shape
    rows = c_all.shape[0]
    tn = 1024
    return pl.pallas_call(
        _mod_kernel,
        grid=(depth, n6 // tn),
        in_specs=[pl.BlockSpec((rows, d), lambda l, j: (0, 0)),
                  pl.BlockSpec((1, d, tn), lambda l, j: (l, 0, j)),
                  pl.BlockSpec((1, 1, tn), lambda l, j: (l, 0, j))],
        out_specs=pl.BlockSpec((1, rows, tn), lambda l, j: (l, 0, j)),
        out_shape=jax.ShapeDtypeStruct((depth, rows, n6), F32),
        compiler_params=_cparams(("arbitrary", "arbitrary")),
        name="modulation",
    )(c_all, w_mod, b_mod.reshape(depth, 1, n6))


def _mod_spec(row_mod, tm, k):
    if row_mod:
        return pl.BlockSpec((1, tm, D_MODEL), lambda b, i: (b, i, k))
    return pl.BlockSpec((1, 1, D_MODEL), lambda b, i: (b, 0, k))


def _rms(x, g):
    ms = jnp.mean(x * x, axis=-1, keepdims=True)
    return x * lax.rsqrt(ms + EPS) * g


def _prenorm_kernel(*refs, has_resid):
    if has_resid:
        x_ref, r_ref, gate_ref, sh_ref, sc_ref, g_ref, xo_ref, h_ref = refs
        x = x_ref[0] + gate_ref[0] * r_ref[0]
        xo_ref[0] = x
    else:
        x_ref, sh_ref, sc_ref, g_ref, h_ref = refs
        x = x_ref[0]
    h_ref[0] = (_rms(x, g_ref[...]) * (1.0 + sc_ref[0]) + sh_ref[0]).astype(BF16)


def _prenorm(x, resid, mod, prev_mod, row_mod, g, tm):
    nb, t, d = x.shape
    has_resid = resid is not None
    xspec = pl.BlockSpec((1, tm, d), lambda b, i: (b, i, 0))
    in_specs, args = [xspec], [x]
    if has_resid:
        in_specs += [xspec, _mod_spec(row_mod, tm, 5)]
        args += [resid, prev_mod]
    in_specs += [_mod_spec(row_mod, tm, 0), _mod_spec(row_mod, tm, 1),
                 pl.BlockSpec((1, d), lambda b, i: (0, 0))]
    args += [mod, mod, g.reshape(1, d)]
    hshape = jax.ShapeDtypeStruct((nb, t, d), BF16)
    if has_resid:
        out_specs, out_shape = (xspec, xspec), (jax.ShapeDtypeStruct((nb, t, d), F32), hshape)
    else:
        out_specs, out_shape = xspec, hshape
    out = pl.pallas_call(
        functools.partial(_prenorm_kernel, has_resid=has_resid),
        grid=(nb, t // tm),
        in_specs=in_specs, out_specs=out_specs, out_shape=out_shape,
        compiler_params=_cparams(("arbitrary", "arbitrary")),
        name="prenorm",
    )(*args)
    return (out[1], out[0]) if has_resid else (out, x)


def _proj_kernel(h_ref, w_ref, o_ref):
    o_ref[0] = jnp.dot(h_ref[0], w_ref[...], preferred_element_type=F32).astype(o_ref.dtype)


def _proj(h, w, out_dtype, tm):
    nb, t, d = h.shape
    n = w.shape[1]
    tn = PROJ_TN
    return pl.pallas_call(
        _proj_kernel,
        grid=(n // tn, nb, t // tm),
        in_specs=[pl.BlockSpec((1, tm, d), lambda j, b, i: (b, i, 0)),
                  pl.BlockSpec((d, tn), lambda j, b, i: (0, j))],
        out_specs=pl.BlockSpec((1, tm, tn), lambda j, b, i: (b, i, j)),
        out_shape=jax.ShapeDtypeStruct((nb, t, n), out_dtype),
        compiler_params=_cparams(("arbitrary", "arbitrary", "arbitrary")),
        name="inproj",
    )(h, w)


def _conv_kernel(a_ref, b_ref, ha_ref, hb_ref, init_ref, wdw_ref, bdw_ref, lng_ref, lnb_ref,
                 wpw_ref, y_ref, tail_ref, full_ref, cv_ref, *, tm):
    i = pl.program_id(1)
    full_ref[HALO:HALO + tm, :] = a_ref[0].astype(F32) * _sigmoid(b_ref[0].astype(F32))

    @pl.when(i == 0)
    def _():
        full_ref[0:HALO, :] = init_ref[0]

    @pl.when(i > 0)
    def _():
        full_ref[0:HALO, :] = ha_ref[0].astype(F32) * _sigmoid(hb_ref[0].astype(F32))

    tail_ref[0] = full_ref[tm:tm + HALO, :]

    cc = 256
    rb = min(tm, 64)
    off = HALO - (CONV_W - 1)
    for c0 in range(0, D_MODEL, cc):
        for r0 in range(0, tm, rb):
            acc = jnp.zeros((rb, cc), F32)
            for s in range(SUBLANES):
                taps = [j for j in range(CONV_W) if (off + j) % SUBLANES == s]
                span = max((off + j) // SUBLANES for j in taps) * SUBLANES + rb
                win = full_ref[r0 + s:r0 + s + span, c0:c0 + cc]
                for j in taps:
                    a0 = (off + j) // SUBLANES * SUBLANES
                    wj = jnp.concatenate([wdw_ref[j, :, c0:c0 + cc]] * (rb // SUBLANES), axis=0)
                    acc = acc + wj * win[a0:a0 + rb]
            cv_ref[r0:r0 + rb, c0:c0 + cc] = acc + bdw_ref[:, c0:c0 + cc]

    y = cv_ref[...]
    mu = jnp.mean(y, axis=-1, keepdims=True)
    yc = y - mu
    var = jnp.mean(yc * yc, axis=-1, keepdims=True)
    yn = yc * lax.rsqrt(var + EPS) * lng_ref[...] + lnb_ref[...]
    y_ref[0] = jnp.dot(_silu(yn).astype(BF16), wpw_ref[...], preferred_element_type=F32)


def _conv_branch(zb, init_buf, w_dw, b_dw, ln_g, ln_b, w_pw, tm):
    nb, t, _ = zb.shape
    d = D_MODEL
    hb = tm // HALO
    tile = lambda k: pl.BlockSpec((1, tm, d), lambda b, i: (b, i, k))
    halo = lambda k: pl.BlockSpec((1, HALO, d), lambda b, i: (b, jnp.maximum(i * hb - 1, 0), k))
    vec = pl.BlockSpec((1, d), lambda b, i: (0, 0))
    return pl.pallas_call(
        functools.partial(_conv_kernel, tm=tm),
        grid=(nb, t // tm),
        in_specs=[tile(C_GA // d), tile(C_GB // d), halo(C_GA // d), halo(C_GB // d),
                  pl.BlockSpec((1, HALO, d), lambda b, i: (b, 0, 0)),
                  pl.BlockSpec((CONV_W, SUBLANES, d), lambda b, i: (0, 0, 0)),
                  vec, vec, vec, _resident((d, d), lambda b, i: (0, 0))],
        out_specs=(pl.BlockSpec((1, tm, d), lambda b, i: (b, i, 0)),
                   pl.BlockSpec((1, HALO, d), lambda b, i: (b, 0, 0))),
        out_shape=(jax.ShapeDtypeStruct((nb, t, d), F32),
                   jax.ShapeDtypeStruct((nb, HALO, d), F32)),
        scratch_shapes=[pltpu.VMEM((tm + HALO, d), F32), pltpu.VMEM((tm, d), F32)],
        compiler_params=_cparams(("arbitrary", "arbitrary")),
        name="conv_branch",
    )(zb, zb, zb, zb, init_buf, w_dw, b_dw.reshape(1, d), ln_g.reshape(1, d), ln_b.reshape(1, d), w_pw)


def _attn_kernel(q_ref, qi_ref, kw_ref, k_hbm, v_hbm, ki_ref, o_ref,
                 keys_ref, qs_ref, qis_ref, wb_ref, m_ref, acc_ref, sc_ref, kbuf, vbuf, kv_sem,
                 *, tq, kb, s_pad, causal, s_valid, n_sel):
    bi = pl.program_id(0)
    i = pl.program_id(1)
    q0 = i * tq
    gq = N_HEADS // N_KV_HEADS
    nlb = kb // LANES
    kbs = 2 * kb
    if causal:
        nkb = (q0 + tq + kb - 1) // kb
        nkbs = (q0 + tq + kbs - 1) // kbs
    else:
        nkb = s_pad // kb
        nkbs = s_pad // kbs

    def k_copy(blk):
        c0 = pl.multiple_of(blk * kb, kb)
        slot = lax.rem(blk, 3)
        return pltpu.make_async_copy(k_hbm.at[bi, pl.ds(c0, kb), :], kbuf.at[slot], kv_sem.at[0, slot])

    def v_copy(blk):
        c0 = pl.multiple_of(blk * kb, kb)
        slot = lax.rem(blk, 2)
        return pltpu.make_async_copy(v_hbm.at[bi, pl.ds(c0, kb), :], vbuf.at[slot], kv_sem.at[1, slot])

    k_copy(0).start()
    v_copy(0).start()

    @pl.when(nkb > 1)
    def _():
        k_copy(1).start()

    for h in range(N_HEADS):
        g, j = divmod(h, gq)
        qs_ref[g, j * tq:(j + 1) * tq, :] = q_ref[0, :, h * HEAD_DIM:(h + 1) * HEAD_DIM].astype(BF16)
    kw = kw_ref[0]
    for h in range(N_IDX_HEADS):
        qis_ref[h * tq:(h + 1) * tq, :] = (
            qi_ref[0, :, h * IDX_DIM:(h + 1) * IDX_DIM] * (IDX_DIM ** -0.5)).astype(BF16)
        wcol = kw[:, IDX_DIM + h:IDX_DIM + h + 1] * (N_IDX_HEADS ** -0.5)
        wb_ref[h] = jnp.broadcast_to(wcol, (tq, LANES))

    rows = q0 + lax.broadcasted_iota(I32, (tq, 1), 0)
    if causal:
        limit = ((rows >> CHUNK_SHIFT) + 1) << CHUNK_SHIFT
    else:
        limit = jnp.full((tq, 1), s_valid, I32)

    sub = 256

    def score_block(b, carry):
        m1, m2 = carry
        for u in range(kbs // sub):
            c0 = pl.multiple_of(b * kbs + u * sub, sub)
            d = lax.dot_general(qis_ref[...], ki_ref[0, pl.ds(c0, sub), :], (((1,), (1,)), ((), ())),
                                preferred_element_type=F32)
            acc = jnp.zeros((tq, sub), F32)
            for h in range(N_IDX_HEADS):
                w = jnp.concatenate([wb_ref[h]] * (sub // LANES), axis=1)
                acc = acc + w * jnp.maximum(d[h * tq:(h + 1) * tq], 0.0)
            pos = c0 + lax.broadcasted_iota(I32, (1, sub), 1)
            acc = jnp.where(pos < limit, acc, -jnp.inf)
            for v in range(sub // LANES):
                x = acc[:, v * LANES:(v + 1) * LANES]
                m2 = jnp.maximum(m2, jnp.minimum(m1, x))
                m1 = jnp.maximum(m1, x)
            keys_ref[:, pl.ds(c0, sub)] = _f2k(acc)
        return m1, m2

    neg = jnp.full((tq, LANES), -jnp.inf, F32)
    m1, m2 = lax.fori_loop(0, nkbs, score_block, (neg, neg))

    def count_ge(thr):
        def body(b, c):
            blk = keys_ref[:, pl.ds(pl.multiple_of(b * kb, kb), kb)]
            hit = jnp.where(blk >= thr, 1.0, 0.0)
            for u in range(nlb):
                c = c + hit[:, u * LANES:(u + 1) * LANES]
            return c
        c = lax.fori_loop(0, nkb, body, jnp.zeros((tq, LANES), F32))
        return jnp.sum(c, axis=1, keepdims=True)

    nsf = float(n_sel)
    log_n = math.log(n_sel)
    one = jnp.ones((tq, 1), F32)

    def live(lo, hi, settled):
        return jnp.where(settled < 0.5, jnp.where(hi > lo + 1, 1.0, 0.0), 0.0)

    def search_cond(st):
        it, lo, hi, settled = st[0], st[1], st[2], st[8]
        return jnp.logical_and(it < 100, jnp.max(live(lo, hi, settled)) > 0.5)

    def search_step(st):
        it, lo, hi, clo, chi, wlo, whi, last, settled, thr = st
        flo = (jnp.log(clo + 0.5) - log_n) * wlo
        fhi = (jnp.log(chi + 0.5) - log_n) * whi
        ck = _f2k((_k2f(lo) * (-fhi) + _k2f(hi) * flo) / (flo - fhi))
        mid = (lo & hi) + ((lo ^ hi) >> 1)
        inside = jnp.where(ck > lo, jnp.where(ck < hi, 1.0, 0.0), 0.0)
        take_mid = jnp.where(lax.rem(it, 3) == 2, 0.0, inside) < 0.5
        cand = jnp.where(take_mid, mid, ck)
        cnt = count_ge(cand)
        act = live(lo, hi, settled)
        up_lo = jnp.where(cnt >= nsf, act, 0.0) > 0.5
        up_hi = jnp.where(cnt >= nsf, 0.0, act) > 0.5
        hit = jnp.where(cnt == nsf, act, 0.0) > 0.5
        thr = jnp.where(hit, cand, thr)
        settled = jnp.where(hit, 1.0, settled)
        whi_n = jnp.where(up_lo, jnp.where(last == 1.0, whi * 0.5, whi), jnp.where(up_hi, 1.0, whi))
        wlo_n = jnp.where(up_hi, jnp.where(last == 2.0, wlo * 0.5, wlo), jnp.where(up_lo, 1.0, wlo))
        last = jnp.where(up_lo, 1.0, jnp.where(up_hi, 2.0, last))
        return (it + 1, jnp.where(up_lo, cand, lo), jnp.where(up_hi, cand, hi),
                jnp.where(up_lo, cnt, clo), jnp.where(up_hi, cnt, chi), wlo_n, whi_n, last, settled, thr)

    st = lax.while_loop(search_cond, search_step, (
        jnp.int32(0),
        _f2k(jnp.min(m2, axis=1, keepdims=True)),
        _f2k(jnp.max(m1, axis=1, keepdims=True)) + 1,
        2.0 * nsf * one, 0.0 * one, one, one, 0.0 * one,
        jnp.where(limit <= n_sel, 1.0, 0.0),
        jnp.full((tq, 1), NEG_INF_KEY + 1, I32)))
    settled = st[8]
    thr = jnp.maximum(jnp.where(settled > 0.5, st[9], st[1]), NEG_INF_KEY + 1)

    @pl.when(jnp.min(settled) < 0.5)
    def _():
        excess = count_ge(thr) - float(n_sel)

        @pl.when(jnp.max(excess) > 0.0)
        def _():
            def count_eq(b, c):
                blk = keys_ref[:, pl.ds(pl.multiple_of(b * kb, kb), kb)]
                hit = jnp.where(blk == thr, 1.0, 0.0)
                for u in range(nlb):
                    c = c + hit[:, u * LANES:(u + 1) * LANES]
                return c
            n_eq = jnp.sum(lax.fori_loop(0, nkb, count_eq, jnp.zeros((tq, LANES), F32)),
                           axis=1, keepdims=True)
            keep = jnp.where(excess > 0.0, n_eq - excess, n_eq)
            tri = jnp.where(lax.broadcasted_iota(I32, (LANES, LANES), 0)
                            <= lax.broadcasted_iota(I32, (LANES, LANES), 1), 1.0, 0.0).astype(BF16)

            def drop_block(b, run):
                c0 = pl.multiple_of(b * LANES, LANES)
                blk = keys_ref[:, pl.ds(c0, LANES)]
                eq = blk == thr
                eqf = jnp.where(eq, 1.0, 0.0)
                rank = run + jnp.dot(eqf.astype(BF16), tri, preferred_element_type=F32) - 1.0
                keys_ref[:, pl.ds(c0, LANES)] = jnp.where(eq & (rank >= keep), INT_MIN, blk)
                return run + jnp.sum(eqf, axis=1, keepdims=True)

            lax.fori_loop(0, nkb * nlb, drop_block, jnp.zeros((tq, 1), F32))

    m_ref[...] = jnp.full(m_ref.shape, NEG_BIG, F32)
    acc_ref[...] = jnp.zeros(acc_ref.shape, F32)
    scale = (HEAD_DIM ** -0.5) * LOG2_E
    ones = jnp.ones((kb, HEAD_DIM), BF16)

    def qk(g, blk):
        kg = kbuf[lax.rem(blk, 3), :, g * HEAD_DIM:(g + 1) * HEAD_DIM]
        return lax.dot_general(qs_ref[g], kg, (((1,), (1,)), ((), ())), preferred_element_type=F32)

    k_copy(0).wait()
    for g in range(N_KV_HEADS):
        sc_ref[g] = qk(g, 0)

    def attend_block(b, carry):
        c0 = pl.multiple_of(b * kb, kb)
        nxt = jnp.minimum(b + 1, nkb - 1)

        @pl.when(b + 1 < nkb)
        def _():
            k_copy(b + 1).wait()

        v_copy(b).wait()

        @pl.when(b + 2 < nkb)
        def _():
            k_copy(b + 2).start()

        @pl.when(b + 1 < nkb)
        def _():
            v_copy(b + 1).start()

        bias = jnp.where(keys_ref[:, pl.ds(c0, kb)] >= thr, 0.0, NEG_BIG)
        bias = jnp.concatenate([bias] * gq, axis=0)
        vslot = lax.rem(b, 2)
        for g in range(N_KV_HEADS):
            s = sc_ref[g] * scale + bias
            sc_ref[g] = qk(g, nxt)
            vg = vbuf[vslot, :, g * HEAD_DIM:(g + 1) * HEAD_DIM]
            m_prev = m_ref[g]
            m_new = jnp.maximum(m_prev, jnp.max(s, axis=1, keepdims=True))
            alpha = jnp.exp2(m_prev - m_new)
            p = jnp.exp2(s - jnp.concatenate([m_new] * nlb, axis=1))
            pv = jnp.dot(p.astype(BF16), jnp.concatenate([vg, ones], axis=1),
                         preferred_element_type=F32)
            acc_ref[g] = jnp.concatenate([alpha, alpha], axis=1) * acc_ref[g] + pv
            m_ref[g] = m_new
        return carry

    lax.fori_loop(0, nkb, attend_block, 0)

    for h in range(N_HEADS):
        g, j = divmod(h, gq)
        o = acc_ref[g, j * tq:(j + 1) * tq, 0:HEAD_DIM] / acc_ref[g, j * tq:(j + 1) * tq, HEAD_DIM:2 * HEAD_DIM]
        o_ref[0, :, h * HEAD_DIM:(h + 1) * HEAD_DIM] = o.astype(BF16)


def _attention(zb, zkv, kb16, vb16, ki16, tq, kb, causal, s_valid):
    nb, t, _ = zb.shape
    s_pad = kb16.shape[1]
    assert s_pad % (2 * kb) == 0
    assert min(TOPK_MAX, s_valid // 4) <= 2 * LANES
    dq = N_HEADS * HEAD_DIM
    dqi = N_IDX_HEADS * IDX_DIM
    dkv = N_KV_HEADS * HEAD_DIM
    gq = N_HEADS // N_KV_HEADS
    return pl.pallas_call(
        functools.partial(_attn_kernel, tq=tq, kb=kb, s_pad=s_pad, causal=causal, s_valid=s_valid,
                          n_sel=min(TOPK_MAX, s_valid // 4)),
        grid=(nb, t // tq),
        in_specs=[pl.BlockSpec((1, tq, dq), lambda b, i: (b, i, C_Q // dq)),
                  pl.BlockSpec((1, tq, dqi), lambda b, i: (b, i, C_QI // dqi)),
                  pl.BlockSpec((1, tq, LANES), lambda b, i: (b, i, C_KIWI // LANES)),
                  pl.BlockSpec(memory_space=pl.ANY),
                  pl.BlockSpec(memory_space=pl.ANY),
                  _resident((1, s_pad, IDX_DIM), lambda b, i: (b, 0, 0))],
        out_specs=pl.BlockSpec((1, tq, dq), lambda b, i: (b, i, 0)),
        out_shape=jax.ShapeDtypeStruct((nb, t, dq), BF16),
        scratch_shapes=[pltpu.VMEM((tq, s_pad), I32),
                        pltpu.VMEM((N_KV_HEADS, gq * tq, HEAD_DIM), BF16),
                        pltpu.VMEM((N_IDX_HEADS * tq, IDX_DIM), BF16),
                        pltpu.VMEM((N_IDX_HEADS, tq, LANES), F32),
                        pltpu.VMEM((N_KV_HEADS, gq * tq, LANES), F32),
                        pltpu.VMEM((N_KV_HEADS, gq * tq, 2 * HEAD_DIM), F32),
                        pltpu.VMEM((N_KV_HEADS, gq * tq, kb), F32),
                        pltpu.VMEM((3, kb, dkv), BF16),
                        pltpu.VMEM((2, kb, dkv), BF16),
                        pltpu.SemaphoreType.DMA((2, 3))],
        compiler_params=_cparams(("arbitrary", "arbitrary")),
        name="sparse_attention",
    )(zb, zb, zkv, kb16, vb16, ki16)


def _post_kernel(o_ref, yc_ref, ga_ref, gb_ref, x_ref, g1_ref, sh2_ref, sc2_ref, n2_ref,
                 wao_ref, wout_ref, xm_ref, h2_ref):
    ya = jnp.dot(o_ref[0], wao_ref[...], preferred_element_type=F32)
    m = _sigmoid(ga_ref[0].astype(F32)) * yc_ref[0] + _sigmoid(gb_ref[0].astype(F32)) * ya
    mix = jnp.dot(m.astype(BF16), wout_ref[...], preferred_element_type=F32)
    x = x_ref[0] + g1_ref[0] * mix
    xm_ref[0] = x
    h2_ref[0] = (_rms(x, n2_ref[...]) * (1.0 + sc2_ref[0]) + sh2_ref[0]).astype(BF16)


def _post(o, y_conv, zb, x, mod, row_mod, n2_g, w_ao, w_out, tm):
    nb, t, d = x.shape
    tile = lambda k: pl.BlockSpec((1, tm, d), lambda b, i: (b, i, k))
    return pl.pallas_call(
        _post_kernel,
        grid=(nb, t // tm),
        in_specs=[tile(0), tile(0), tile(C_GTA // d), tile(C_GTB // d), tile(0),
                  _mod_spec(row_mod, tm, 2), _mod_spec(row_mod, tm, 3),
                  _mod_spec(row_mod, tm, 4),
                  pl.BlockSpec((1, d), lambda b, i: (0, 0)),
                  _resident((d, d), lambda b, i: (0, 0)),
                  _resident((d, d), lambda b, i: (0, 0))],
        out_specs=(tile(0), tile(0)),
        out_shape=(jax.ShapeDtypeStruct((nb, t, d), F32), jax.ShapeDtypeStruct((nb, t, d), BF16)),
        compiler_params=_cparams(("arbitrary", "arbitrary")),
        name="merge_out_proj",
    )(o, y_conv, zb, zb, x, mod, mod, mod, n2_g.reshape(1, d), w_ao, w_out)


def _router_kernel(h_ref, wrt_ref, rb_ref, dest_ref, wts_ref, meta_ref, *, tm):
    lt = lax.dot_general(wrt_ref[...], h_ref[...], (((1,), (1,)), ((), ())),
                         preferred_element_type=F32) + rb_ref[...][:, 0:1]
    row = lambda r: lt[r:r + 1, :]
    best, gi = row(0), jnp.zeros((1, tm), I32)
    for k in range(1, N_GROUPS):
        upd = row(k) > best
        best = jnp.where(upd, row(k), best)
        gi = jnp.where(upd, k, gi)
    sg = jnp.zeros((1, tm), F32)
    for k in range(N_GROUPS):
        sg = sg + jnp.exp(row(k) - best)
    p_group = 1.0 / sg

    le = []
    for j in range(EXPERTS_PER_GROUP):
        v = row(N_GROUPS + j)
        for g in range(1, N_GROUPS):
            v = jnp.where(gi == g, row(N_GROUPS + g * EXPERTS_PER_GROUP + j), v)
        le.append(v)
    b1, i1 = le[0], jnp.zeros((1, tm), I32)
    for j in range(1, EXPERTS_PER_GROUP):
        upd = le[j] > b1
        b1 = jnp.where(upd, le[j], b1)
        i1 = jnp.where(upd, j, i1)
    b2, i2 = jnp.full((1, tm), -jnp.inf, F32), jnp.zeros((1, tm), I32)
    for j in range(EXPERTS_PER_GROUP):
        upd = jnp.where(i1 == j, -jnp.inf, le[j]) > b2
        b2 = jnp.where(upd, le[j], b2)
        i2 = jnp.where(upd, j, i2)
    p2 = jnp.exp(b2 - b1)
    w0 = p_group / (1.0 + p2)
    w1 = p_group * p2 / (1.0 + p2)
    e0 = gi * EXPERTS_PER_GROUP + i1
    e1 = gi * EXPERTS_PER_GROUP + i2

    eio = lax.broadcasted_iota(I32, (N_EXPERTS, tm), 0)
    hit0 = jnp.where(eio == e0, 1.0, 0.0)
    hit1 = jnp.where(eio == e1, 1.0, 0.0)
    et = hit0 + hit1
    counts = jnp.sum(et, axis=1, keepdims=True)
    units = jnp.floor((counts + (BF16_ROWS - 1)) * (1.0 / BF16_ROWS))
    lower = jnp.where(lax.broadcasted_iota(I32, (N_EXPERTS, N_EXPERTS), 1)
                      < lax.broadcasted_iota(I32, (N_EXPERTS, N_EXPERTS), 0), 1.0, 0.0)
    starts = jnp.dot(lower.astype(BF16), jnp.broadcast_to(units, (N_EXPERTS, LANES)).astype(BF16),
                     preferred_element_type=F32) * float(BF16_ROWS)
    before = jnp.where(lax.broadcasted_iota(I32, (tm, tm), 0)
                       < lax.broadcasted_iota(I32, (tm, tm), 1), 1.0, 0.0).astype(BF16)
    ranks = jnp.dot(et.astype(BF16), before, preferred_element_type=F32)
    slot = starts[:, 0:1] + ranks
    d0 = jnp.sum(hit0 * slot, axis=0, keepdims=True)
    d1 = jnp.sum(hit1 * slot, axis=0, keepdims=True)
    dest_ref[0] = jnp.zeros((8, tm), I32)
    dest_ref[0, 0:1, :] = d0.astype(I32)
    dest_ref[0, 1:2, :] = d1.astype(I32)
    wts_ref[0] = jnp.zeros((8, tm), F32)
    wts_ref[0, 0:1, :] = w0
    wts_ref[0, 1:2, :] = w1
    meta_ref[0, 0:N_EXPERTS, :] = starts.astype(I32)
    meta_ref[0, N_EXPERTS:2 * N_EXPERTS, :] = jnp.broadcast_to(counts, (N_EXPERTS, LANES)).astype(I32)


def _router(h2, wrt, rbias, tm):
    n, d = h2.shape
    nt = n // tm
    return pl.pallas_call(
        functools.partial(_router_kernel, tm=tm),
        grid=(nt,),
        in_specs=[pl.BlockSpec((tm, d), lambda i: (i, 0)),
                  pl.BlockSpec((LANES, d), lambda i: (0, 0)),
                  pl.BlockSpec((LANES, LANES), lambda i: (0, 0))],
        out_specs=(pl.BlockSpec((1, 8, tm), lambda i: (i, 0, 0)),
                   pl.BlockSpec((1, 8, tm), lambda i: (i, 0, 0)),
                   pl.BlockSpec((1, 2 * N_EXPERTS, LANES), lambda i: (i, 0, 0))),
        out_shape=(jax.ShapeDtypeStruct((nt, 8, tm), I32),
                   jax.ShapeDtypeStruct((nt, 8, tm), F32),
                   jax.ShapeDtypeStruct((nt, 2 * N_EXPERTS, LANES), I32)),
        compiler_params=_cparams(("arbitrary",)),
        name="moe_router",
    )(h2, wrt, rbias)


def _experts_kernel(meta_ref, h_ref, dest_ref, wts_ref, wg_ref, wu_ref, wd_ref, o_ref,
                    dall_ref, ys_ref, ws_ref, *, tm, nslot, rb):
    t = pl.program_id(0)
    e = pl.program_id(1)
    sb = 256

    @pl.when(e == 0)
    def _():
        d0 = dest_ref[0, 0:1, :]
        d1 = dest_ref[0, 1:2, :]
        w0 = wts_ref[0, 0:1, :]
        w1 = wts_ref[0, 1:2, :]
        for s0 in range(0, nslot, sb):
            sio = s0 + lax.broadcasted_iota(I32, (sb, 1), 0)
            m0 = sio == d0
            m1 = sio == d1
            dall_ref[s0:s0 + sb, :] = (jnp.where(m0, 1.0, 0.0) + jnp.where(m1, 1.0, 0.0)).astype(BF16)
            wcol = jnp.sum(jnp.where(m0, w0, 0.0) + jnp.where(m1, w1, 0.0), axis=1, keepdims=True)
            ws_ref[s0:s0 + sb, :] = jnp.broadcast_to(wcol, (sb, LANES))
        dall_ref[nslot:nslot + rb, :] = jnp.zeros((rb, tm), BF16)
        ws_ref[nslot:nslot + rb, :] = jnp.zeros((rb, LANES), F32)
        ys_ref[...] = jnp.zeros(ys_ref.shape, BF16)

    start = meta_ref[t, e]
    n = meta_ref[t, N_EXPERTS + e]

    def run_block(j, carry):
        r0 = pl.multiple_of(start + j * rb, BF16_ROWS)
        xe = jnp.dot(dall_ref[pl.ds(r0, rb), :], h_ref[...], preferred_element_type=F32).astype(BF16)
        g = jnp.dot(xe, wg_ref[0], preferred_element_type=F32)
        u = jnp.dot(xe, wu_ref[0], preferred_element_type=F32)
        y = jnp.dot((_silu(g) * u).astype(BF16), wd_ref[0], preferred_element_type=F32)
        y = y * jnp.concatenate([ws_ref[pl.ds(r0, rb), :]] * (D_MODEL // LANES), axis=1)
        ys_ref[pl.ds(r0, rb), :] = y.astype(BF16)
        return carry

    lax.fori_loop(0, (n + rb - 1) // rb, run_block, 0)

    @pl.when(e == N_EXPERTS - 1)
    def _():
        cw = 512
        for c0 in range(0, D_MODEL, cw):
            o_ref[:, c0:c0 + cw] = lax.dot_general(
                dall_ref[0:nslot, :], ys_ref[0:nslot, c0:c0 + cw], (((0,), (0,)), ((), ())),
                preferred_element_type=F32)


def _experts(h2, dest, wts, meta, w_g, w_u, w_d, tm):
    n, d = h2.shape
    nt = n // tm
    rb = 128
    nslot = -(-(2 * tm + N_EXPERTS * (BF16_ROWS - 1)) // 256) * 256
    grid_spec = pltpu.PrefetchScalarGridSpec(
        num_scalar_prefetch=1,
        grid=(nt, N_EXPERTS),
        in_specs=[_resident((tm, d), lambda t, e, m: (t, 0)),
                  pl.BlockSpec((1, 8, tm), lambda t, e, m: (t, 0, 0)),
                  pl.BlockSpec((1, 8, tm), lambda t, e, m: (t, 0, 0)),
                  pl.BlockSpec((1, d, D_EXPERT), lambda t, e, m: (e, 0, 0)),
                  pl.BlockSpec((1, d, D_EXPERT), lambda t, e, m: (e, 0, 0)),
                  pl.BlockSpec((1, D_EXPERT, d), lambda t, e, m: (e, 0, 0))],
        out_specs=pl.BlockSpec((tm, d), lambda t, e, m: (t, 0)),
        scratch_shapes=[pltpu.VMEM((nslot + rb, tm), BF16),
                        pltpu.VMEM((nslot + rb, d), BF16),
                        pltpu.VMEM((nslot + rb, LANES), F32)])
    return pl.pallas_call(
        functools.partial(_experts_kernel, tm=tm, nslot=nslot, rb=rb),
        grid_spec=grid_spec,
        out_shape=jax.ShapeDtypeStruct((n, d), F32),
        compiler_params=_cparams(("arbitrary", "arbitrary")),
        name="moe_experts",
    )(meta, h2, dest, wts, w_g, w_u, w_d)


def _moe(h2, wrt, rbias, w_g, w_u, w_d, tm):
    dest, wts, meta = _router(h2, wrt, rbias, tm)
    return _experts(h2, dest, wts, meta[:, :, 0], w_g, w_u, w_d, tm)


def _final_kernel(x_ref, r_ref, gate_ref, g_ref, y_ref):
    y_ref[0] = _rms(x_ref[0] + gate_ref[0] * r_ref[0], g_ref[...])


def _final(x, resid, mod, row_mod, g, tm):
    nb, t, d = x.shape
    tile = pl.BlockSpec((1, tm, d), lambda b, i: (b, i, 0))
    return pl.pallas_call(
        _final_kernel,
        grid=(nb, t // tm),
        in_specs=[tile, tile, _mod_spec(row_mod, tm, 5), pl.BlockSpec((1, d), lambda b, i: (0, 0))],
        out_specs=tile,
        out_shape=jax.ShapeDtypeStruct((nb, t, d), F32),
        compiler_params=_cparams(("arbitrary", "arbitrary")),
        name="final_norm",
    )(x, resid, mod, g.reshape(1, d))


def _split_w_in(w):
    d = w.shape[0]
    parts_b = [w[:, 0:6144], w[:, 8272:12368], w[:, 7168:8192]]
    parts_kv = [w[:, 6144:7168], w[:, 8192:8272]]
    pad = lambda parts, n: parts + [jnp.zeros((d, n - sum(p.shape[1] for p in parts)), w.dtype)]
    return (jnp.concatenate(pad(parts_b, N_COLS_B), axis=1).astype(BF16),
            jnp.concatenate(pad(parts_kv, N_COLS_KV), axis=1).astype(BF16))


def _pick_tile(t, pref):
    tm = min(t, pref)
    assert t % tm == 0
    return tm


def kernel(x_prompt, x_sample, c_prompt, c_sample, cache_k, cache_v, cache_kidx, state_conv, norm1_g, norm2_g, w_mod, b_mod, w_in, conv_dw_w, conv_dw_b, conv_ln_g, conv_ln_b, w_conv_out, w_attn_out, w_out, router_g, router_g_b, router_e, router_e_b, w_gate, w_up, w_down, final_g):
    depth = w_in.shape[0]
    bp, seq, d = x_prompt.shape
    bs, dseq, _ = x_sample.shape
    past = cache_k.shape[2]
    n_keep = min(seq, past)
    dkv = N_KV_HEADS * HEAD_DIM
    kb = 512

    nc = bp + bs
    c_all = jnp.concatenate([c_prompt, c_sample, jnp.zeros((-nc % 8, d), F32)], axis=0)
    mod_all = _modulation(c_all, w_mod, b_mod)

    xp = x_prompt
    xs = x_sample.reshape(1, bs * dseq, d)
    res_p = res_s = None
    prev_mp = prev_ms = None
    outs = [[] for _ in range(8)]
    zero_buf = jnp.zeros((bp, HALO, d), F32)
    s_all = past + dseq
    s_pad = -(-s_all // (2 * kb)) * (2 * kb)

    for l in range(depth):
        mod_p = mod_all[l, :bp][:, None, :]
        mod_s = jnp.repeat(mod_all[l, bp:nc], dseq, axis=0)[None]
        wz_b, wz_kv = _split_w_in(w_in[l])
        w_pw = w_conv_out[l].astype(BF16)
        w_ao = w_attn_out[l].astype(BF16)
        w_o = w_out[l].astype(BF16)
        w_dw = jnp.broadcast_to(conv_dw_w[l][:, None, :], (CONV_W, SUBLANES, d))
        wrt = jnp.concatenate([router_g[l].T, router_e[l].T,
                               jnp.zeros((LANES - N_GROUPS - N_EXPERTS, d), F32)], axis=0).astype(BF16)
        rbias = jnp.broadcast_to(jnp.concatenate(
            [router_g_b[l], router_e_b[l], jnp.zeros((LANES - N_GROUPS - N_EXPERTS,), F32)])[:, None],
            (LANES, LANES))
        wg, wu, wd = w_gate[l].astype(BF16), w_up[l].astype(BF16), w_down[l].astype(BF16)

        hp, xp = _prenorm(xp, res_p, mod_p, prev_mp, False, norm1_g[l], _pick_tile(seq, 512))
        zp = _proj(hp, wz_b, BF16, _pick_tile(seq, 1024))
        zkv_p = _proj(hp, wz_kv, F32, _pick_tile(seq, 1024))
        yc_p, tail_p = _conv_branch(zp, zero_buf, w_dw, conv_dw_b[l], conv_ln_g[l], conv_ln_b[l],
                                    w_pw, _pick_tile(seq, 256))
        k_p = zkv_p[:, :, C_K:C_K + dkv]
        v_p = zkv_p[:, :, C_V:C_V + dkv]
        ki_p = zkv_p[:, :, C_KIWI:C_KIWI + IDX_DIM]
        o_p = _attention(zp, zkv_p, k_p.astype(BF16), v_p.astype(BF16), ki_p.astype(BF16),
                         _pick_tile(seq, 128), kb, True, seq)
        xm_p, h2_p = _post(o_p, yc_p, zp, xp, mod_p, False, norm2_g[l], w_ao, w_o, _pick_tile(seq, 256))
        res_p = _moe(h2_p.reshape(bp * seq, d), wrt, rbias, wg, wu, wd,
                     _pick_tile(bp * seq, 1024)).reshape(bp, seq, d)
        xp, prev_mp = xm_p, mod_p

        hs, xs = _prenorm(xs, res_s, mod_s, prev_ms, True, norm1_g[l], bs * dseq)
        zs = _proj(hs, wz_b, BF16, bs * dseq)
        zs_b = zs.reshape(bs, dseq, N_COLS_B)
        zkv_s = _proj(hs, wz_kv, F32, bs * dseq).reshape(bs, dseq, N_COLS_KV)
        init_s = jnp.concatenate([jnp.zeros((bs, HALO - (CONV_W - 1), d), F32), state_conv[l]], axis=1)
        yc_s, tail_s = _conv_branch(zs_b, init_s, w_dw, conv_dw_b[l], conv_ln_g[l], conv_ln_b[l],
                                    w_pw, dseq)
        k_s = zkv_s[:, :, C_K:C_K + dkv]
        v_s = zkv_s[:, :, C_V:C_V + dkv]
        ki_s = zkv_s[:, :, C_KIWI:C_KIWI + IDX_DIM]
        padk = jnp.zeros((bs, s_pad - s_all, dkv), BF16)
        k_all = jnp.concatenate([cache_k[l].reshape(bs, past, dkv).astype(BF16), k_s.astype(BF16), padk], axis=1)
        v_all = jnp.concatenate([cache_v[l].reshape(bs, past, dkv).astype(BF16), v_s.astype(BF16), padk], axis=1)
        ki_all = jnp.concatenate([cache_kidx[l].astype(BF16), ki_s.astype(BF16),
                                  jnp.zeros((bs, s_pad - s_all, IDX_DIM), BF16)], axis=1)
        o_s = _attention(zs_b, zkv_s, k_all, v_all, ki_all, dseq, kb, False, s_all)
        xm_s, h2_s = _post(o_s.reshape(1, bs * dseq, d), yc_s.reshape(1, bs * dseq, d), zs, xs,
                           mod_s, True, norm2_g[l], w_ao, w_o, bs * dseq)
        res_s = _moe(h2_s.reshape(bs * dseq, d), wrt, rbias, wg, wu, wd,
                     bs * dseq).reshape(1, bs * dseq, d)
        xs, prev_ms = xm_s, mod_s

        outs[0].append(k_p[:, seq - n_keep:].reshape(bp, n_keep, N_KV_HEADS, HEAD_DIM))
        outs[1].append(v_p[:, seq - n_keep:].reshape(bp, n_keep, N_KV_HEADS, HEAD_DIM))
        outs[2].append(ki_p[:, seq - n_keep:])
        outs[3].append(tail_p[:, HALO - (CONV_W - 1):])
        outs[4].append(k_s.reshape(bs, dseq, N_KV_HEADS, HEAD_DIM))
        outs[5].append(v_s.reshape(bs, dseq, N_KV_HEADS, HEAD_DIM))
        outs[6].append(ki_s)
        outs[7].append(tail_s[:, HALO - (CONV_W - 1):])

    y_p = _final(xp, res_p, prev_mp, False, final_g, _pick_tile(seq, 512))
    y_s = _final(xs, res_s, prev_ms, True, final_g, bs * dseq).reshape(bs, dseq, d)
    return (y_p, y_s) + tuple(jnp.stack(o) for o in outs)
```
